```python
import math
import jax, jax.numpy as jnp
from jax import lax
import numpy as np

D_MODEL = 2048
BATCH = 2
SEQ = 8192
DEPTH = 1
DEC_BATCH = 32
DEC_SEQ = 4
PAST_LEN = 16384
PAGE_SIZE = 128

N_HEADS = 8
HEAD_DIM = D_MODEL // (2 * N_HEADS)
V_DIM = 2 * HEAD_DIM
D_ATTN = N_HEADS * V_DIM
QK_COLS = N_HEADS * 2 * HEAD_DIM
D_CONV = D_MODEL
CONV_W = 3
IN_COLS = 2 * QK_COLS + D_ATTN + 3 * D_CONV + D_ATTN + D_CONV
SPLIT_POINTS = (QK_COLS, 2 * QK_COLS, 2 * QK_COLS + D_ATTN,
                2 * QK_COLS + D_ATTN + D_CONV, 2 * QK_COLS + D_ATTN + 2 * D_CONV,
                2 * QK_COLS + D_ATTN + 3 * D_CONV, 2 * QK_COLS + 2 * D_ATTN + 3 * D_CONV)
NUM_BUCKETS = 32
MAX_DISTANCE = 128
Q_BLOCK = 128
N_EXPERTS = 32
TOP_K = 4
D_FF = D_MODEL
SWIGLU_LIMIT = 7.0
SWIGLU_ALPHA = 1.702
MOE_BLOCK = 128
EPS = 1e-6

kernel_name = "hybrid_diffattn_shortconv_moe_step"


def rms_norm(x, g):
    xf = x.astype(jnp.float32)
    y = xf * lax.rsqrt(jnp.mean(xf * xf, axis=-1, keepdims=True) + EPS) * g.astype(jnp.float32)
    return y.astype(x.dtype)


def t5_bucket(rel):
    n = jnp.maximum(-rel, 0)
    max_exact = NUM_BUCKETS // 2
    large = max_exact + (jnp.log(jnp.maximum(n, 1).astype(jnp.float32) / max_exact)
                         / math.log(MAX_DISTANCE / max_exact) * (NUM_BUCKETS - max_exact)).astype(jnp.int32)
    large = jnp.minimum(large, NUM_BUCKETS - 1)
    return jnp.where(n < max_exact, n, large)


def rel_bias_and_mask(table, q_pos, k_pos):
    rel = k_pos[None, :] - q_pos[:, None]
    bias = table[t5_bucket(rel)]
    return jnp.transpose(bias, (2, 0, 1)).astype(jnp.float32), rel <= 0


def diff_attend(q, k, v, bias, mask, lam):
    s = jnp.einsum('...hmqd,...hmkd->...hmqk', q, k).astype(jnp.float32) * (HEAD_DIM ** -0.5)
    s = jnp.where(mask, s + bias[:, None], -jnp.inf)
    p = jax.nn.softmax(s, axis=-1)
    a = p[..., 0, :, :] - lam * p[..., 1, :, :]
    return jnp.einsum('...hqk,...hkv->...qhv', a.astype(v.dtype), v)


def prompt_attention(q, k, v, table, lam):
    b, s = q.shape[0], q.shape[1]
    qt = jnp.transpose(q, (0, 2, 3, 1, 4))
    kt = jnp.transpose(k, (0, 2, 3, 1, 4))
    vt = jnp.transpose(v, (0, 2, 1, 3))
    k_pos = jnp.arange(s, dtype=jnp.int32)

    def block(i):
        start = i * Q_BLOCK
        qb = lax.dynamic_slice_in_dim(qt, start, Q_BLOCK, axis=3)
        bias, mask = rel_bias_and_mask(table, start + jnp.arange(Q_BLOCK, dtype=jnp.int32), k_pos)
        return diff_attend(qb, kt, vt, bias, mask, lam)

    out = lax.map(block, jnp.arange(s // Q_BLOCK, dtype=jnp.int32))
    return jnp.transpose(out, (1, 0, 2, 3, 4)).reshape(b, s, N_HEADS, V_DIM)


def sample_attention(q, k, v, cache_k, cache_v, layer, page_table, table, lam):
    t = q.shape[1]
    past = page_table.shape[1] * PAGE_SIZE
    bias, mask = rel_bias_and_mask(table, past + jnp.arange(t, dtype=jnp.int32),
                                   jnp.arange(past + t, dtype=jnp.int32))

    def one_seq(args):
        qs, ks, vs, pages = args
        kp = cache_k[layer, pages].reshape(past, N_HEADS, 2, HEAD_DIM).astype(ks.dtype)
        vp = cache_v[layer, pages].reshape(past, N_HEADS, V_DIM).astype(vs.dtype)
        kall = jnp.concatenate([kp, ks], axis=0)
        vall = jnp.concatenate([vp, vs], axis=0)
        return diff_attend(jnp.transpose(qs, (1, 2, 0, 3)), jnp.transpose(kall, (1, 2, 0, 3)),
                           jnp.transpose(vall, (1, 0, 2)), bias, mask, lam)

    return lax.map(one_seq, (q, k, v, page_table))


def short_conv(u, prev, w):
    t = u.shape[1]
    up = jnp.concatenate([prev.astype(u.dtype), u], axis=1)
    y = up[:, 0:t] * w[0]
    for j in range(1, CONV_W):
        y = y + up[:, j:j + t] * w[j]
    return y, up[:, up.shape[1] - (CONV_W - 1):]


def project(x, norm_g, w_in):
    b, t = x.shape[0], x.shape[1]
    z = jnp.einsum('btd,dc->btc', rms_norm(x, norm_g), w_in)
    q, k, v, cb, cc, ch, ga, gb = jnp.split(z, SPLIT_POINTS, axis=-1)
    q = q.reshape(b, t, N_HEADS, 2, HEAD_DIM)
    k = k.reshape(b, t, N_HEADS, 2, HEAD_DIM)
    v = v.reshape(b, t, N_HEADS, V_DIM)
    return q, k, v, cb, cc * ch, ga, gb


def merge(x, attn, conv, ga, gb, subln_g, lam_init, w_out):
    b, t = x.shape[0], x.shape[1]
    attn = (rms_norm(attn, subln_g) * (1.0 - lam_init)).reshape(b, t, D_ATTN)
    m = jax.nn.sigmoid(ga) * attn + jax.nn.sigmoid(gb) * conv
    return x + jnp.einsum('btc,cd->btd', m, w_out)


def moe_ffn(h, layer, router_w, router_b, w_gate_up, b_gate_up, w_down, b_down):
    n, d = h.shape
    logits = jnp.einsum('nd,de->ne', h, router_w[layer]).astype(jnp.float32) + router_b[layer].astype(jnp.float32)
    top_logit, top_idx = lax.top_k(logits, TOP_K)
    gates = jax.nn.softmax(top_logit, axis=-1)
    flat_e = top_idx.reshape(-1)
    flat_t = jnp.repeat(jnp.arange(n, dtype=jnp.int32), TOP_K)
    flat_g = gates.reshape(-1)
    order = jnp.argsort(flat_e)
    se = flat_e[order]
    counts = jnp.bincount(flat_e, length=N_EXPERTS)
    padded = (counts + MOE_BLOCK - 1) // MOE_BLOCK * MOE_BLOCK
    pad_end = jnp.cumsum(padded)
    pad_start = pad_end - padded
    start = jnp.cumsum(counts) - counts
    dest = pad_start[se] + jnp.arange(n * TOP_K, dtype=jnp.int32) - start[se]
    n_blocks = -(-(n * TOP_K) // MOE_BLOCK) + N_EXPERTS
    n_rows = n_blocks * MOE_BLOCK
    row_tok = jnp.full((n_rows,), n, dtype=jnp.int32).at[dest].set(flat_t[order])
    row_gate = jnp.zeros((n_rows,), jnp.float32).at[dest].set(flat_g[order])
    block_e = jnp.minimum(jnp.searchsorted(pad_end, jnp.arange(n_blocks) * MOE_BLOCK, side='right'),
                          N_EXPERTS - 1).astype(jnp.int32)
    h_pad = jnp.concatenate([h, jnp.zeros((1, d), h.dtype)], axis=0)
    xb = h_pad[row_tok].reshape(n_blocks, MOE_BLOCK, d)

    def expert_block(args):
        xe, e = args
        gu = xe @ w_gate_up[layer, e] + b_gate_up[layer, e]
        gate = jnp.minimum(gu[:, 0::2], SWIGLU_LIMIT)
        up = jnp.clip(gu[:, 1::2], -SWIGLU_LIMIT, SWIGLU_LIMIT)
        act = (up + 1.0) * gate * jax.nn.sigmoid(SWIGLU_ALPHA * gate)
        return act @ w_down[layer, e] + b_down[layer, e]

    yb = lax.map(expert_block, (xb, block_e)).reshape(n_rows, d)
    y = jnp.zeros((n + 1, d), yb.dtype).at[row_tok].add(yb * row_gate[:, None].astype(yb.dtype))
    return y[:n]


def setup_inputs(seed: int = 0) -> dict:
    key = jax.random.key(seed)
    ks = jax.random.split(key, 26)
    f32 = jnp.float32
    n_pages = PAST_LEN // PAGE_SIZE
    n_used = DEC_BATCH * n_pages
    n_pool = n_used + n_used // 4
    nrm = lambda k, shape, s: jax.random.normal(k, shape, f32) * s
    page_table = jax.random.permutation(ks[5], n_pool)[:n_used].reshape(DEC_BATCH, n_pages).astype(jnp.int32)
    return {
        "x_prompt": nrm(ks[0], (BATCH, SEQ, D_MODEL), 1.0),
        "x_sample": nrm(ks[1], (DEC_BATCH, DEC_SEQ, D_MODEL), 1.0),
        "cache_k": nrm(ks[2], (DEPTH, n_pool, PAGE_SIZE, N_HEADS, 2 * HEAD_DIM), 1.0),
        "cache_v": nrm(ks[3], (DEPTH, n_pool, PAGE_SIZE, N_HEADS, V_DIM), 1.0),
        "state_conv": nrm(ks[4], (DEPTH, DEC_BATCH, CONV_W - 1, D_CONV), 1.0),
        "page_table": page_table,
        "attn_norm_g": 1.0 + nrm(ks[6], (DEPTH, D_MODEL), 0.02),
        "w_in": nrm(ks[7], (DEPTH, D_MODEL, IN_COLS), D_MODEL ** -0.5),
        "lambda_q1": nrm(ks[8], (DEPTH, HEAD_DIM), 0.1),
        "lambda_k1": nrm(ks[9], (DEPTH, HEAD_DIM), 0.1),
        "lambda_q2": nrm(ks[10], (DEPTH, HEAD_DIM), 0.1),
        "lambda_k2": nrm(ks[11], (DEPTH, HEAD_DIM), 0.1),
        "subln_g": 1.0 + nrm(ks[12], (DEPTH, V_DIM), 0.02),
        "rel_bias_table": nrm(ks[13], (NUM_BUCKETS, N_HEADS), 0.2),
        "conv_w": nrm(ks[14], (DEPTH, CONV_W, D_CONV), CONV_W ** -0.5),
        "w_out": nrm(ks[15], (DEPTH, D_ATTN, D_MODEL), D_ATTN ** -0.5),
        "ffn_norm_g": 1.0 + nrm(ks[16], (DEPTH, D_MODEL), 0.02),
        "router_w": nrm(ks[17], (DEPTH, D_MODEL, N_EXPERTS), D_MODEL ** -0.5),
        "router_b": nrm(ks[18], (DEPTH, N_EXPERTS), 0.01),
        "w_gate_up": nrm(ks[19], (DEPTH, N_EXPERTS, D_MODEL, 2 * D_FF), D_MODEL ** -0.5),
        "b_gate_up": nrm(ks[20], (DEPTH, N_EXPERTS, 2 * D_FF), 0.01),
        "w_down": nrm(ks[21], (DEPTH, N_EXPERTS, D_FF, D_MODEL), D_FF ** -0.5),
        "b_down": nrm(ks[22], (DEPTH, N_EXPERTS, D_MODEL), 0.01),
        "final_norm_g": 1.0 + nrm(ks[23], (D_MODEL,), 0.02),
    }


def reference(x_prompt, x_sample, cache_k, cache_v, state_conv, page_table, attn_norm_g, w_in,
              lambda_q1, lambda_k1, lambda_q2, lambda_k2, subln_g, rel_bias_table, conv_w, w_out,
              ffn_norm_g, router_w, router_b, w_gate_up, b_gate_up, w_down, b_down, final_norm_g):
    xp, xs = x_prompt, x_sample
    bp, tp = xp.shape[0], xp.shape[1]
    bs, ts = xs.shape[0], xs.shape[1]
    kp_rows, vp_rows, ks_rows, vs_rows, cp_states, cs_states = [], [], [], [], [], []
    for l in range(DEPTH):
        lam_init = 0.8 - 0.6 * math.exp(-0.3 * l)
        lam = (jnp.exp(jnp.sum(lambda_q1[l] * lambda_k1[l]).astype(jnp.float32))
               - jnp.exp(jnp.sum(lambda_q2[l] * lambda_k2[l]).astype(jnp.float32)) + lam_init)
        q, k, v, gate_b, u, ga, gb = project(xp, attn_norm_g[l], w_in[l])
        attn = prompt_attention(q, k, v, rel_bias_table, lam)
        conv, conv_state = short_conv(u, jnp.zeros((bp, CONV_W - 1, D_CONV), u.dtype), conv_w[l])
        xp = merge(xp, attn, gate_b * conv, ga, gb, subln_g[l], lam_init, w_out[l])
        xp = xp + moe_ffn(rms_norm(xp, ffn_norm_g[l]).reshape(bp * tp, D_MODEL), l, router_w, router_b,
                          w_gate_up, b_gate_up, w_down, b_down).reshape(bp, tp, D_MODEL)
        kp_rows.append(k.reshape(bp, tp, N_HEADS, 2 * HEAD_DIM))
        vp_rows.append(v)
        cp_states.append(conv_state)
        q, k, v, gate_b, u, ga, gb = project(xs, attn_norm_g[l], w_in[l])
        attn = sample_attention(q, k, v, cache_k, cache_v, l, page_table, rel_bias_table, lam)
        conv, conv_state = short_conv(u, state_conv[l], conv_w[l])
        xs = merge(xs, attn, gate_b * conv, ga, gb, subln_g[l], lam_init, w_out[l])
        xs = xs + moe_ffn(rms_norm(xs, ffn_norm_g[l]).reshape(bs * ts, D_MODEL), l, router_w, router_b,
                          w_gate_up, b_gate_up, w_down, b_down).reshape(bs, ts, D_MODEL)
        ks_rows.append(k.reshape(bs, ts, N_HEADS, 2 * HEAD_DIM))
        vs_rows.append(v)
        cs_states.append(conv_state)
    y_prompt = rms_norm(xp, final_norm_g)
    y_sample = rms_norm(xs, final_norm_g)
    return (y_prompt, y_sample, jnp.stack(kp_rows), jnp.stack(vp_rows), jnp.stack(ks_rows),
            jnp.stack(vs_rows), jnp.stack(cp_states), jnp.stack(cs_states))
```

```python
import functools
import math

import jax
import jax.numpy as jnp
from jax import lax
from jax.experimental import pallas as pl
from jax.experimental.pallas import tpu as pltpu

F32 = jnp.float32
BF16 = jnp.bfloat16
I32 = jnp.int32

EPS = 1e-6
MAX_DISTANCE = 128
TOP_K = 4
SWIGLU_LIMIT = 7.0
SWIGLU_ALPHA = 1.702
MASKED = -1e30
LOG2E = math.log2(math.e)
N_SECTIONS = 8

VMEM_LIMIT = 48 * 1024 * 1024
PROJ_TM, PROJ_TN = 512, 256
ATTN_TB = 512
DECODE_PAGES_PER_STEP = 4
MERGE_TM = 512
MOE_TM, MOE_FC = 512, 512
DISPATCH_TT = 512
COMBINE_TC = 128


def _params(sem):
    return pltpu.CompilerParams(dimension_semantics=sem, vmem_limit_bytes=VMEM_LIMIT)


def _dot(a, b):
    return jnp.dot(a, b, preferred_element_type=F32)


def _dot_nt(a, b):
    return lax.dot_general(a, b, (((1,), (1,)), ((), ())), preferred_element_type=F32)


def _proj_kernel(*refs, carry_mode, t_tiles, t_len, q_scale):
    x_ref, g_ref = refs[0:2]
    w_refs = refs[2:2 + N_SECTIONS]
    cw_ref = refs[2 + N_SECTIONS]
    if carry_mode:
        prev_ref = refs[3 + N_SECTIONS]
        outs = refs[4 + N_SECTIONS:]
    else:
        p0_ref, p1_ref = refs[3 + N_SECTIONS:5 + N_SECTIONS]
        outs = refs[5 + N_SECTIONS:]
    q_out, k_out, v_out, kb_out, vb_out, mconv_out, sga_out, st_out, h_scr, carry_scr = outs
    i = pl.program_id(0)
    j = pl.program_id(1)

    @pl.when(j == 0)
    def _():
        x = x_ref[...]
        ms = jnp.mean(x * x, axis=-1, keepdims=True)
        h_scr[...] = (x * lax.rsqrt(ms + EPS) * g_ref[...]).astype(BF16)

    h = h_scr[...]
    wq, wk, wv, wcb, wcc, wch, wga, wgb = w_refs
    q_out[...] = (_dot(h, wq[...]) * q_scale).astype(BF16)
    k = _dot(h, wk[...])
    k_out[...] = k
    kb_out[...] = k.astype(BF16)
    v = _dot(h, wv[...])
    v_out[...] = v
    vb_out[...] = v.astype(BF16)
    sga_out[...] = jax.nn.sigmoid(_dot(h, wga[...])).astype(BF16)

    u = _dot(h, wcc[...]) * _dot(h, wch[...])
    tm = u.shape[0]
    r = lax.broadcasted_iota(I32, u.shape, 0)
    d1 = pltpu.roll(u, 1, 0)
    d2 = pltpu.roll(u, 2, 0)
    if carry_mode:
        @pl.when(i % t_tiles == 0)
        def _():
            carry_scr[j] = prev_ref[0]
        c = carry_scr[j]
        c0, c1 = c[0:1, :], c[1:2, :]
        s1 = jnp.where(r == 0, c1, d1)
        s2 = jnp.where(r == 0, c0, jnp.where(r == 1, c1, d2))
        carry_scr[j] = u[tm - 2:tm, :]
        st_out[0] = u[tm - 2:tm, :]
    else:
        t = r % t_len
        p0, p1 = p0_ref[...], p1_ref[...]
        s1 = jnp.where(t == 0, p1, d1)
        s2 = jnp.where(t == 0, p0, jnp.where(t == 1, p1, d2))
        st_out[...] = u
    cw = cw_ref[...]
    conv = s2 * cw[0:1, :] + s1 * cw[1:2, :] + u * cw[2:3, :]
    gate_b = _dot(h, wcb[...])
    mconv_out[...] = (jax.nn.sigmoid(_dot(h, wgb[...])) * gate_b * conv).astype(BF16)


def _rms_proj(x2d, norm_g, w_bf, conv_w, prev, *, t_len, head_dim):
    n, d = x2d.shape
    tm, tn = min(PROJ_TM, n), PROJ_TN
    assert n % tm == 0 and d % tn == 0 and t_len >= 2
    carry_mode = t_len % tm == 0
    n_col = d // tn
    row_blk = pl.BlockSpec((tm, tn), lambda i, j: (i, j))
    in_specs = [pl.BlockSpec((tm, d), lambda i, j: (i, 0)),
                pl.BlockSpec((1, d), lambda i, j: (0, 0))]
    in_specs += [pl.BlockSpec((d, tn), functools.partial(lambda i, j, s: (0, s * n_col + j), s=s))
                 for s in range(N_SECTIONS)]
    in_specs.append(pl.BlockSpec((3, tn), lambda i, j: (0, j)))
    if carry_mode:
        t_tiles = t_len // tm
        in_specs.append(pl.BlockSpec((1, 2, tn), lambda i, j: (i // t_tiles, 0, j)))
        prev_args = (prev,)
        st_shape = jax.ShapeDtypeStruct((n // tm, 2, d), F32)
        st_spec = pl.BlockSpec((1, 2, tn), lambda i, j: (i, 0, j))
    else:
        assert tm % t_len == 0
        t_tiles = 1
        in_specs += [row_blk, row_blk]
        prev_args = tuple(prev)
        st_shape = jax.ShapeDtypeStruct((n, d), F32)
        st_spec = row_blk
    out_shape = [jax.ShapeDtypeStruct((n, d), BF16), jax.ShapeDtypeStruct((n, d), F32),
                 jax.ShapeDtypeStruct((n, d), F32), jax.ShapeDtypeStruct((n, d), BF16),
                 jax.ShapeDtypeStruct((n, d), BF16), jax.ShapeDtypeStruct((n, d), BF16),
                 jax.ShapeDtypeStruct((n, d), BF16), st_shape]
    kern = functools.partial(_proj_kernel, carry_mode=carry_mode, t_tiles=t_tiles, t_len=t_len,
                             q_scale=head_dim ** -0.5 * LOG2E)
    return pl.pallas_call(
        kern,
        grid=(n // tm, n_col),
        in_specs=in_specs,
        out_specs=[row_blk] * 7 + [st_spec],
        out_shape=out_shape,
        scratch_shapes=[pltpu.VMEM((tm, d), BF16), pltpu.VMEM((n_col, 2, tn), F32)],
        compiler_params=_params(("arbitrary", "arbitrary")),
        name="rms_proj",
    )(x2d, norm_g.reshape(1, d), *([w_bf] * N_SECTIONS), conv_w, *prev_args)


def _t5_bias(table, dist):
    nb = table.shape[0]
    n = jnp.maximum(dist, 0)
    max_exact = nb // 2
    large = max_exact + (jnp.log(jnp.maximum(n, 1).astype(F32) / max_exact)
                         / math.log(MAX_DISTANCE / max_exact) * (nb - max_exact)).astype(I32)
    bucket = jnp.where(n < max_exact, n, jnp.minimum(large, nb - 1))
    bias = jnp.moveaxis(table[bucket].astype(F32), -1, 0) * LOG2E
    return jnp.where(dist >= 0, bias, MASKED)


def _pattn_kernel(lam_ref, cfar_ref, q_ref, k_ref, v_ref, bd_ref, bs_ref, g_ref, o_ref,
                  m_scr, l_scr, a_scr, *, tb, hd, out_scale):
    h = pl.program_id(1)
    qi = pl.program_id(2)
    q = q_ref[...]
    qs = (q[:, :hd], q[:, hd:])
    cfar = cfar_ref[h]

    m_scr[...] = jnp.full(m_scr.shape, MASKED, F32)
    l_scr[...] = jnp.zeros(l_scr.shape, F32)
    a_scr[...] = jnp.zeros(a_scr.shape, F32)

    def block(j, bias):
        start = pl.multiple_of(j * tb, tb)
        kk = k_ref[pl.ds(start, tb), :]
        vv = v_ref[pl.ds(start, tb), :]
        for mi in range(2):
            s = _dot_nt(qs[mi], kk[:, mi * hd:(mi + 1) * hd])
            s = s + (cfar if bias is None else bias)
            m_old = m_scr[mi]
            m_new = jnp.maximum(m_old, jnp.max(s, axis=-1, keepdims=True))
            p = jnp.exp2(s - m_new)
            alpha = jnp.exp2(m_old - m_new)
            l_scr[mi] = alpha * l_scr[mi] + jnp.sum(p, axis=-1, keepdims=True)
            a_scr[mi] = alpha * a_scr[mi] + _dot(p.astype(BF16), vv)
            m_scr[mi] = m_new

    def far(j, carry):
        block(j, None)
        return carry

    lax.fori_loop(0, jnp.maximum(qi - 1, 0), far, 0)

    @pl.when(qi >= 1)
    def _():
        block(qi - 1, bs_ref[0])

    block(qi, bd_ref[0])

    o = a_scr[0] / l_scr[0] - lam_ref[0] * (a_scr[1] / l_scr[1])
    ms = jnp.mean(o * o, axis=-1, keepdims=True)
    o_ref[...] = (o * lax.rsqrt(ms + EPS) * g_ref[...] * out_scale).astype(o_ref.dtype)


def _prompt_attn(q, kb, vb, table, lam, subln_g, *, b, t, n_heads, hd, lam_init):
    n, d = q.shape
    tb = min(ATTN_TB, t)
    vd = 2 * hd
    assert t % tb == 0 and tb + 1 >= MAX_DISTANCE
    nq = t // tb
    pos = jnp.arange(tb, dtype=I32)
    bias_diag = _t5_bias(table, pos[:, None] - pos[None, :])
    bias_sub = _t5_bias(table, tb + pos[:, None] - pos[None, :])
    cfar = table[table.shape[0] - 1].astype(F32) * LOG2E
    kern = functools.partial(_pattn_kernel, tb=tb, hd=hd, out_scale=1.0 - lam_init)
    smem = pl.BlockSpec(memory_space=pltpu.SMEM)
    return pl.pallas_call(
        kern,
        grid=(b, n_heads, nq),
        in_specs=[smem, smem,
                  pl.BlockSpec((tb, vd), lambda bi, h, i: (bi * nq + i, h)),
                  pl.BlockSpec((t, vd), lambda bi, h, i: (bi, h)),
                  pl.BlockSpec((t, vd), lambda bi, h, i: (bi, h)),
                  pl.BlockSpec((1, tb, tb), lambda bi, h, i: (h, 0, 0)),
                  pl.BlockSpec((1, tb, tb), lambda bi, h, i: (h, 0, 0)),
                  pl.BlockSpec((1, vd), lambda bi, h, i: (0, 0))],
        out_specs=pl.BlockSpec((tb, vd), lambda bi, h, i: (bi * nq + i, h)),
        out_shape=jax.ShapeDtypeStruct((n, d), BF16),
        scratch_shapes=[pltpu.VMEM((2, tb, 1), F32), pltpu.VMEM((2, tb, 1), F32),
                        pltpu.VMEM((2, tb, vd), F32)],
        compiler_params=_params(("arbitrary", "arbitrary", "arbitrary")),
        name="prompt_attn",
    )(lam, cfar, q, kb, vb, bias_diag, bias_sub, subln_g.reshape(1, vd))


def _decode_kernel(pt_ref, lam_ref, q_ref, *refs, n_pages, n_heads, vd, tq, out_scale):
    k_refs = refs[:n_pages]
    v_refs = refs[n_pages:2 * n_pages]
    (bias_ref, kn_ref, vn_ref, bn_ref, g_ref, o_ref, m_scr, l_scr, a_scr) = refs[2 * n_pages:]
    c = pl.program_id(1)
    page = k_refs[0].shape[1]

    @pl.when(c == 0)
    def _():
        for h in range(n_heads):
            cols = slice(h * vd, (h + 1) * vd)
            s = _dot_nt(q_ref[0, h], kn_ref[0, :, cols].astype(BF16)) + bn_ref[h]
            m = jnp.max(s, axis=-1, keepdims=True)
            p = jnp.exp2(s - m)
            m_scr[h] = m
            l_scr[h] = jnp.sum(p, axis=-1, keepdims=True)
            a_scr[h] = _dot(p.astype(BF16), vn_ref[0, :, cols].astype(BF16))

    for h in range(n_heads):
        cols = slice(h * vd, (h + 1) * vd)
        qh = q_ref[0, h]
        s = jnp.concatenate([_dot_nt(qh, k_refs[i][0, :, cols].astype(BF16))
                             for i in range(n_pages)], axis=1) + bias_ref[0, h]
        m_old = m_scr[h]
        m_new = jnp.maximum(m_old, jnp.max(s, axis=-1, keepdims=True))
        p = jnp.exp2(s - m_new).astype(BF16)
        alpha = jnp.exp2(m_old - m_new)
        pv = _dot(p[:, 0:page], v_refs[0][0, :, cols].astype(BF16))
        for i in range(1, n_pages):
            pv = pv + _dot(p[:, i * page:(i + 1) * page], v_refs[i][0, :, cols].astype(BF16))
        l_scr[h] = alpha * l_scr[h] + jnp.sum(p.astype(F32), axis=-1, keepdims=True)
        a_scr[h] = alpha * a_scr[h] + pv
        m_scr[h] = m_new

    @pl.when(c == pl.num_programs(1) - 1)
    def _():
        for h in range(n_heads):
            a = a_scr[h]
            l = l_scr[h]
            o = a[0:tq] / l[0:tq] - lam_ref[0] * (a[tq:2 * tq] / l[tq:2 * tq])
            ms = jnp.mean(o * o, axis=-1, keepdims=True)
            o_ref[0, :, h * vd:(h + 1) * vd] = o * lax.rsqrt(ms + EPS) * g_ref[...] * out_scale


def _decode_attn(q, k_new, v_new, cache_k2, cache_v2, page_table, page_base, table, lam, subln_g,
                 *, db, tq, n_heads, hd, lam_init):
    vd = 2 * hd
    d = n_heads * vd
    page = cache_k2.shape[1]
    np_seq = page_table.shape[1]
    cpp = DECODE_PAGES_PER_STEP
    assert np_seq % cpp == 0
    n_chunks = np_seq // cpp
    past = np_seq * page
    rows = 2 * tq

    q5 = q.reshape(db, tq, n_heads, 2, hd).transpose(0, 2, 3, 1, 4)
    z = jnp.zeros_like(q5[:, :, 0])
    q_bd = jnp.concatenate([jnp.concatenate([q5[:, :, 0], z], -1),
                            jnp.concatenate([z, q5[:, :, 1]], -1)], axis=2)

    t_q = jnp.tile(jnp.arange(tq, dtype=I32), 2)
    kpos = jnp.arange(past, dtype=I32)
    bias_past = _t5_bias(table, past + t_q[:, None] - kpos[None, :])
    bias_past = bias_past.reshape(n_heads, rows, n_chunks, cpp * page).transpose(2, 0, 1, 3)
    t_k = jnp.arange(rows, dtype=I32)
    dist_new = jnp.where(t_k[None, :] < tq, t_q[:, None] - t_k[None, :], -1)
    bias_new = _t5_bias(table, dist_new)
    pad = ((0, 0), (0, rows - tq), (0, 0))
    kn = jnp.pad(k_new.reshape(db, tq, d), pad)
    vn = jnp.pad(v_new.reshape(db, tq, d), pad)

    def page_map(i):
        return lambda b, c, pt: (page_base + pt[b * np_seq + c * cpp + i], 0, 0)

    page_specs = [pl.BlockSpec((1, page, d), page_map(i)) for i in range(cpp)]
    kern = functools.partial(_decode_kernel, n_pages=cpp, n_heads=n_heads, vd=vd, tq=tq,
                             out_scale=1.0 - lam_init)
    grid_spec = pltpu.PrefetchScalarGridSpec(
        num_scalar_prefetch=1,
        grid=(db, n_chunks),
        in_specs=[pl.BlockSpec(memory_space=pltpu.SMEM),
                  pl.BlockSpec((1, n_heads, rows, vd), lambda b, c, pt: (b, 0, 0, 0))]
                 + page_specs + page_specs
                 + [pl.BlockSpec((1, n_heads, rows, cpp * page), lambda b, c, pt: (c, 0, 0, 0)),
                    pl.BlockSpec((1, rows, d), lambda b, c, pt: (b, 0, 0)),
                    pl.BlockSpec((1, rows, d), lambda b, c, pt: (b, 0, 0)),
                    pl.BlockSpec((n_heads, rows, rows), lambda b, c, pt: (0, 0, 0)),
                    pl.BlockSpec((1, vd), lambda b, c, pt: (0, 0))],
        out_specs=pl.BlockSpec((1, tq, d), lambda b, c, pt: (b, 0, 0)),
        scratch_shapes=[pltpu.VMEM((n_heads, rows, 1), F32), pltpu.VMEM((n_heads, rows, 1), F32),
                        pltpu.VMEM((n_heads, rows, vd), F32)],
    )
    out = pl.pallas_call(
        kern,
        grid_spec=grid_spec,
        out_shape=jax.ShapeDtypeStruct((db, tq, d), F32),
        compiler_params=_params(("arbitrary", "arbitrary")),
        name="decode_attn",
    )(page_table.reshape(-1), lam, q_bd, *([cache_k2] * cpp), *([cache_v2] * cpp),
      bias_past, kn, vn, bias_new, subln_g.reshape(1, vd))
    return out.reshape(db * tq, d).astype(BF16)


def _merge_kernel(attn_ref, sga_ref, mconv_ref, x_ref, wo_ref, g_ref, rwh_ref, rwl_ref, rb_ref,
                  cnt_in_ref, x1_ref, h2_ref, idx_ref, gate_ref, rank_ref, cnt_out_ref, carry_scr):
    i = pl.program_id(0)

    @pl.when(i == 0)
    def _():
        carry_scr[...] = cnt_in_ref[...]

    m = sga_ref[...].astype(F32) * attn_ref[...].astype(F32) + mconv_ref[...].astype(F32)
    x1 = x_ref[...] + _dot(m.astype(BF16), wo_ref[...])
    x1_ref[...] = x1
    ms = jnp.mean(x1 * x1, axis=-1, keepdims=True)
    h2 = x1 * lax.rsqrt(ms + EPS) * g_ref[...]
    h2_ref[...] = h2

    hh = h2.astype(BF16)
    hl = (h2 - hh.astype(F32)).astype(BF16)
    logits = (_dot_nt(rwh_ref[...], hh) + _dot_nt(rwh_ref[...], hl) + _dot_nt(rwl_ref[...], hh)
              + rb_ref[...])
    n_exp, tm = logits.shape
    e_iota = lax.broadcasted_iota(I32, (n_exp, tm), 0).astype(F32)
    work = logits
    tops, sels = [], []
    for k in range(TOP_K):
        mx = jnp.max(work, axis=0, keepdims=True)
        ik = jnp.min(jnp.where(work == mx, e_iota, float(n_exp)), axis=0, keepdims=True)
        sel = e_iota == ik
        work = jnp.where(sel, -jnp.inf, work)
        tops.append(mx)
        sels.append(sel)
        idx_ref[k:k + 1, :] = ik.astype(I32)
    exps = [jnp.exp(t - tops[0]) for t in tops]
    den = exps[0] + exps[1] + exps[2] + exps[3]
    for k in range(TOP_K):
        gate_ref[k:k + 1, :] = exps[k] / den

    assigned = sels[0] | sels[1] | sels[2] | sels[3]
    a_mat = jnp.where(assigned, 1.0, 0.0).astype(BF16)
    rr = lax.broadcasted_iota(I32, (tm, tm), 0)
    cc = lax.broadcasted_iota(I32, (tm, tm), 1)
    upper = jnp.where(rr < cc, 1.0, 0.0).astype(BF16)
    before = _dot(a_mat, upper) + carry_scr[:, 0:1]
    for k in range(TOP_K):
        rank_ref[k:k + 1, :] = jnp.sum(jnp.where(sels[k], before, 0.0), axis=0,
                                       keepdims=True).astype(I32)
    carry_scr[...] = carry_scr[...] + jnp.sum(a_mat.astype(F32), axis=1, keepdims=True)
    cnt_out_ref[...] = carry_scr[...]


def _merge_route(attn, sga, mconv, x2d, wo_bf, ffn_g, rw_hi, rw_lo, rb, cnt_in):
    n, d = x2d.shape
    tm = min(MERGE_TM, n)
    assert n % tm == 0
    n_exp = rw_hi.shape[0]
    row = pl.BlockSpec((tm, d), lambda i: (i, 0))
    full = lambda shape: pl.BlockSpec(shape, lambda i: (0,) * len(shape))
    tok = pl.BlockSpec((TOP_K, tm), lambda i: (0, i))
    return pl.pallas_call(
        _merge_kernel,
        grid=(n // tm,),
        in_specs=[row, row, row, row, full((d, d)), full((1, d)), full((n_exp, d)),
                  full((n_exp, d)), full((n_exp, 1)), full((n_exp, 128))],
        out_specs=[row, row, tok, tok, tok, full((n_exp, 128))],
        out_shape=[jax.ShapeDtypeStruct((n, d), F32), jax.ShapeDtypeStruct((n, d), F32),
                   jax.ShapeDtypeStruct((TOP_K, n), I32), jax.ShapeDtypeStruct((TOP_K, n), F32),
                   jax.ShapeDtypeStruct((TOP_K, n), I32), jax.ShapeDtypeStruct((n_exp, 128), F32)],
        scratch_shapes=[pltpu.VMEM((n_exp, 128), F32)],
        compiler_params=_params(("arbitrary",)),
        name="merge_route",
    )(attn, sga, mconv, x2d, wo_bf, ffn_g.reshape(1, d), rw_hi, rw_lo, rb, cnt_in)


def _tile_indices(dest, tt):
    k, n = dest.shape
    return dest.reshape(k, n // tt, tt).transpose(1, 0, 2).reshape(n // tt, k * tt)


def _dispatch_kernel(dest_ref, h_ref, xs_in_ref, xs_ref, idx_smem, idx_sem, row_sem, *, tt):
    del xs_in_ref
    i = pl.program_id(0)
    cp = pltpu.make_async_copy(dest_ref.at[i], idx_smem, idx_sem)
    cp.start()
    cp.wait()

    def row_copy(r, k):
        return pltpu.make_async_copy(h_ref.at[pl.ds(i * tt + r, 1)],
                                     xs_ref.at[pl.ds(idx_smem[k * tt + r], 1)], row_sem)

    def issue(r, carry):
        for k in range(TOP_K):
            row_copy(r, k).start()
        return carry

    def drain(r, carry):
        for k in range(TOP_K):
            row_copy(r, k).wait()
        return carry

    lax.fori_loop(0, tt, issue, 0)
    lax.fori_loop(0, tt, drain, 0)


def _dispatch(h2, dest, xs):
    n, d = h2.shape
    tt = min(DISPATCH_TT, n)
    assert n % tt == 0
    tiles = _tile_indices(dest, tt)
    return pl.pallas_call(
        functools.partial(_dispatch_kernel, tt=tt),
        grid=(n // tt,),
        in_specs=[pl.BlockSpec(tiles.shape, lambda i: (0, 0)),
                  pl.BlockSpec(memory_space=pl.ANY), pl.BlockSpec(memory_space=pl.ANY)],
        out_specs=pl.BlockSpec(memory_space=pl.ANY),
        out_shape=jax.ShapeDtypeStruct(xs.shape, xs.dtype),
        scratch_shapes=[pltpu.SMEM((TOP_K * tt,), I32), pltpu.SemaphoreType.DMA(()),
                        pltpu.SemaphoreType.DMA(())],
        input_output_aliases={2: 0},
        compiler_params=_params(("arbitrary",)),
        name="dispatch",
    )(tiles, h2, xs)


def _moe_kernel(be_ref, nu_ref, xs_ref, wg_ref, wu_ref, wd_ref, bg_ref, bu_ref, bd_ref, y_ref,
                xb_scr):
    del be_ref
    i = pl.program_id(0)
    f = pl.program_id(1)

    @pl.when(i < nu_ref[0])
    def _():
        @pl.when(f == 0)
        def _():
            xb_scr[...] = xs_ref[...].astype(BF16)

        xb = xb_scr[...]
        gate = jnp.minimum(_dot(xb, wg_ref[0]) + bg_ref[0], SWIGLU_LIMIT)
        up = jnp.clip(_dot(xb, wu_ref[0]) + bu_ref[0], -SWIGLU_LIMIT, SWIGLU_LIMIT)
        act = (up + 1.0) * gate * jax.nn.sigmoid(SWIGLU_ALPHA * gate)
        part = _dot(act.astype(BF16), wd_ref[0])

        @pl.when(f == 0)
        def _():
            y_ref[...] = part + bd_ref[0]

        @pl.when(f > 0)
        def _():
            y_ref[...] += part

    @pl.when((i >= nu_ref[0]) & (f == 0))
    def _():
        y_ref[...] = jnp.zeros(y_ref.shape, y_ref.dtype)


def _moe_ffn(xs, block_e, n_used, wg, wu, wd, bg, bu, bd):
    rows, d = xs.shape
    n_exp, _, ff = wg.shape
    tm, fc = MOE_TM, MOE_FC
    assert rows % tm == 0 and ff % fc == 0
    nf = ff // fc

    def blk(i, nu):
        return jnp.minimum(i, nu[0] - 1)

    def fch(i, f, nu):
        return jnp.where(i < nu[0], f, nf - 1)

    grid_spec = pltpu.PrefetchScalarGridSpec(
        num_scalar_prefetch=2,
        grid=(rows // tm, nf),
        in_specs=[pl.BlockSpec((tm, d), lambda i, f, be, nu: (blk(i, nu), 0)),
                  pl.BlockSpec((1, d, fc), lambda i, f, be, nu: (be[blk(i, nu)], 0, fch(i, f, nu))),
                  pl.BlockSpec((1, d, fc), lambda i, f, be, nu: (be[blk(i, nu)], 0, fch(i, f, nu))),
                  pl.BlockSpec((1, fc, d), lambda i, f, be, nu: (be[blk(i, nu)], fch(i, f, nu), 0)),
                  pl.BlockSpec((1, 1, fc), lambda i, f, be, nu: (be[blk(i, nu)], 0, fch(i, f, nu))),
                  pl.BlockSpec((1, 1, fc), lambda i, f, be, nu: (be[blk(i, nu)], 0, fch(i, f, nu))),
                  pl.BlockSpec((1, 1, d), lambda i, f, be, nu: (be[blk(i, nu)], 0, 0))],
        out_specs=pl.BlockSpec((tm, d), lambda i, f, be, nu: (i, 0)),
        scratch_shapes=[pltpu.VMEM((tm, d), BF16)],
    )
    return pl.pallas_call(
        _moe_kernel,
        grid_spec=grid_spec,
        out_shape=jax.ShapeDtypeStruct((rows, d), F32),
        compiler_params=_params(("arbitrary", "arbitrary")),
        name="moe_ffn",
    )(block_e, n_used, xs, wg, wu, wd, bg, bu, bd)


def _combine_kernel(dest_ref, x1_ref, gate_ref, g_ref, yb_ref, y_ref, buf, idx_smem, idx_sem,
                    row_sem, *, tc):
    i = pl.program_id(0)
    cp = pltpu.make_async_copy(dest_ref.at[i], idx_smem, idx_sem)
    cp.start()
    cp.wait()

    def row_copy(r, k):
        return pltpu.make_async_copy(yb_ref.at[pl.ds(idx_smem[k * tc + r], 1)],
                                     buf.at[k, pl.ds(r, 1)], row_sem)

    def issue(r, carry):
        for k in range(TOP_K):
            row_copy(r, k).start()
        return carry

    def drain(r, carry):
        for k in range(TOP_K):
            row_copy(r, k).wait()
        return carry

    lax.fori_loop(0, tc, issue, 0)
    lax.fori_loop(0, tc, drain, 0)

    gates = gate_ref[...]
    y = x1_ref[...]
    for k in range(TOP_K):
        y = y + gates[:, k:k + 1] * buf[k]
    ms = jnp.mean(y * y, axis=-1, keepdims=True)
    y_ref[...] = y * lax.rsqrt(ms + EPS) * g_ref[...]


def _combine(x1, gates, dest, yb, final_g):
    n, d = x1.shape
    tc = min(COMBINE_TC, n)
    assert n % tc == 0
    tiles = _tile_indices(dest, tc)
    return pl.pallas_call(
        functools.partial(_combine_kernel, tc=tc),
        grid=(n // tc,),
        in_specs=[pl.BlockSpec(tiles.shape, lambda i: (0, 0)),
                  pl.BlockSpec((tc, d), lambda i: (i, 0)),
                  pl.BlockSpec((tc, TOP_K), lambda i: (i, 0)),
                  pl.BlockSpec((1, d), lambda i: (0, 0)),
                  pl.BlockSpec(memory_space=pl.ANY)],
        out_specs=pl.BlockSpec((tc, d), lambda i: (i, 0)),
        out_shape=jax.ShapeDtypeStruct((n, d), F32),
        scratch_shapes=[pltpu.VMEM((TOP_K, tc, d), F32), pltpu.SMEM((TOP_K * tc,), I32),
                        pltpu.SemaphoreType.DMA(()), pltpu.SemaphoreType.DMA(())],
        compiler_params=_params(("arbitrary",)),
        name="combine",
    )(tiles, x1, gates.T, final_g.reshape(1, d), yb)


def kernel(x_prompt, x_sample, cache_k, cache_v, state_conv, page_table, attn_norm_g, w_in,
           lambda_q1, lambda_k1, lambda_q2, lambda_k2, subln_g, rel_bias_table, conv_w, w_out,
           ffn_norm_g, router_w, router_b, w_gate_up, b_gate_up, w_down, b_down, final_norm_g):
    bp, tp, d = x_prompt.shape
    bs, ts, _ = x_sample.shape
    depth, n_pool, page, n_heads, vd = cache_v.shape
    hd = vd // 2
    n_exp = router_w.shape[-1]
    ff = w_down.shape[2]
    np_, ns_ = bp * tp, bs * ts
    cache_k2 = cache_k.reshape(depth * n_pool, page, n_heads * vd)
    cache_v2 = cache_v.reshape(depth * n_pool, page, n_heads * vd)

    xp = x_prompt.reshape(np_, d)
    xs_ = x_sample.reshape(ns_, d)
    kp_rows, vp_rows, ks_rows, vs_rows, cp_states, cs_states = [], [], [], [], [], []
    for l in range(depth):
        lam_init = 0.8 - 0.6 * math.exp(-0.3 * l)
        lam = (jnp.exp(jnp.sum(lambda_q1[l] * lambda_k1[l]).astype(F32))
               - jnp.exp(jnp.sum(lambda_q2[l] * lambda_k2[l]).astype(F32)) + lam_init).reshape(1)
        w_bf = w_in[l].astype(BF16)
        wo_bf = w_out[l].astype(BF16)

        qp, kp, vp, kpb, vpb, mconv_p, sga_p, st_p = _rms_proj(
            xp, attn_norm_g[l], w_bf, conv_w[l], jnp.zeros((bp, 2, d), F32), t_len=tp, head_dim=hd)
        attn_p = _prompt_attn(qp, kpb, vpb, rel_bias_table, lam, subln_g[l], b=bp, t=tp,
                              n_heads=n_heads, hd=hd, lam_init=lam_init)
        st_prev = state_conv[l]
        prev_rows = (jnp.repeat(st_prev[:, 0], ts, axis=0), jnp.repeat(st_prev[:, 1], ts, axis=0))
        qs, ks, vs, _, _, mconv_s, sga_s, u_s = _rms_proj(
            xs_, attn_norm_g[l], w_bf, conv_w[l], prev_rows, t_len=ts, head_dim=hd)
        attn_s = _decode_attn(qs, ks, vs, cache_k2, cache_v2, page_table, l * n_pool,
                              rel_bias_table, lam, subln_g[l], db=bs, tq=ts, n_heads=n_heads,
                              hd=hd, lam_init=lam_init)

        rw_t = router_w[l].T
        rw_hi = rw_t.astype(BF16)
        rw_lo = (rw_t - rw_hi.astype(F32)).astype(BF16)
        rb = router_b[l].astype(F32).reshape(n_exp, 1)
        cnt0 = jnp.zeros((n_exp, 128), F32)
        x1p, h2p, idx_p, gate_p, rank_p, cnt1 = _merge_route(
            attn_p, sga_p, mconv_p, xp, wo_bf, ffn_norm_g[l], rw_hi, rw_lo, rb, cnt0)
        x1s, h2s, idx_s, gate_s, rank_s, cnt2 = _merge_route(
            attn_s, sga_s, mconv_s, xs_, wo_bf, ffn_norm_g[l], rw_hi, rw_lo, rb, cnt1)

        counts = cnt2[:, 0].astype(I32)
        padded = (counts + MOE_TM - 1) // MOE_TM * MOE_TM
        pad_end = jnp.cumsum(padded)
        pad_start = pad_end - padded
        n_blocks = -(-((np_ + ns_) * TOP_K) // MOE_TM) + n_exp
        block_e = jnp.minimum(jnp.searchsorted(pad_end, jnp.arange(n_blocks, dtype=I32) * MOE_TM,
                                               side='right'), n_exp - 1).astype(I32)
        n_used = (pad_end[-1] // MOE_TM).astype(I32).reshape(1)
        dest_p = pad_start[idx_p] + rank_p
        dest_s = pad_start[idx_s] + rank_s

        xsorted = jnp.zeros((n_blocks * MOE_TM, d), F32)
        xsorted = _dispatch(h2p, dest_p, xsorted)
        xsorted = _dispatch(h2s, dest_s, xsorted)

        wgu = w_gate_up[l]
        yb = _moe_ffn(xsorted, block_e, n_used,
                      wgu[:, :, 0::2].astype(BF16), wgu[:, :, 1::2].astype(BF16),
                      w_down[l].astype(BF16),
                      b_gate_up[l][:, 0::2].reshape(n_exp, 1, ff),
                      b_gate_up[l][:, 1::2].reshape(n_exp, 1, ff),
                      b_down[l].reshape(n_exp, 1, d))

        assert l == depth - 1, "only a one-layer stack is supported"
        xp = _combine(x1p, gate_p, dest_p, yb, final_norm_g)
        xs_ = _combine(x1s, gate_s, dest_s, yb, final_norm_g)

        kp_rows.append(kp.reshape(bp, tp, n_heads, vd))
        vp_rows.append(vp.reshape(bp, tp, n_heads, vd))
        cp_states.append(st_p.reshape(bp, -1, 2, d)[:, -1])
        ks_rows.append(ks.reshape(bs, ts, n_heads, vd))
        vs_rows.append(vs.reshape(bs, ts, n_heads, vd))
        cs_states.append(u_s.reshape(bs, ts, d)[:, ts - 2:])
    return (xp.reshape(bp, tp, d), xs_.reshape(bs, ts, d), jnp.stack(kp_rows), jnp.stack(vp_rows),
            jnp.stack(ks_rows), jnp.stack(vs_rows), jnp.stack(cp_states), jnp.stack(cs_states))
```

```python
import functools
import math

import jax
import jax.numpy as jnp
from jax import lax
from jax.experimental import pallas as pl
from jax.experimental.pallas import tpu as pltpu

F32 = jnp.float32
BF16 = jnp.bfloat16
I32 = jnp.int32

EPS = 1e-6
MAX_DISTANCE = 128
TOP_K = 4
SWIGLU_LIMIT = 7.0
SWIGLU_ALPHA = 1.702
MASKED = -1e30
LOG2E = math.log2(math.e)
N_SECTIONS = 8

VMEM_LIMIT = 48 * 1024 * 1024
MXU_HALF = 128
PROJ_TM, PROJ_TN = 512, 256
DECODE_PAGES_PER_STEP = 4
MERGE_TM = 512
MOE_TM, MOE_FC = 512, 1024
MOE_VMEM_LIMIT = 56 * 1024 * 1024
DISPATCH_TT = 512
COMBINE_TC = 128
CAST_BLOCK_BYTES = 4 * 1024 * 1024
SPLIT_TN = 256


def _params(sem, vmem_limit=VMEM_LIMIT):
    return pltpu.CompilerParams(dimension_semantics=sem, vmem_limit_bytes=vmem_limit)


def _dot(a, b):
    return jnp.dot(a, b, preferred_element_type=F32)


def _dot_nt(a, b):
    return lax.dot_general(a, b, (((1,), (1,)), ((), ())), preferred_element_type=F32)


def _cast_kernel(x_ref, o_ref):
    o_ref[...] = x_ref[...].astype(o_ref.dtype)


def _cast_bf16(x2d):
    r, c = x2d.shape
    tr = min(CAST_BLOCK_BYTES // (4 * c), r)
    assert r % tr == 0 and tr % 16 == 0
    return pl.pallas_call(
        _cast_kernel,
        grid=(r // tr,),
        in_specs=[pl.BlockSpec((tr, c), lambda i: (i, 0))],
        out_specs=pl.BlockSpec((tr, c), lambda i: (i, 0)),
        out_shape=jax.ShapeDtypeStruct((r, c), BF16),
        compiler_params=_params(("arbitrary",)),
        name="cast_bf16",
    )(x2d)


def _split_kernel(w_ref, g_ref, u_ref, *, lanes):
    w = w_ref[0].astype(BF16)
    width = 2 * lanes
    rr = lax.broadcasted_iota(I32, (width, width), 0)
    cc = lax.broadcasted_iota(I32, (width, width), 1)
    src = jnp.where(cc < lanes, 2 * cc, 2 * (cc - lanes) + 1)
    perm = jnp.where(rr == src, 1.0, 0.0).astype(BF16)
    for b in range(w.shape[1] // width):
        moved = _dot(w[:, b * width:(b + 1) * width], perm).astype(BF16)
        g_ref[0, :, b * lanes:(b + 1) * lanes] = moved[:, :lanes]
        u_ref[0, :, b * lanes:(b + 1) * lanes] = moved[:, lanes:]


def _split_gate_up(w_gu):
    n_exp, d, ff2 = w_gu.shape
    tn = SPLIT_TN
    assert ff2 % (2 * tn) == 0 and tn % MXU_HALF == 0
    out = jax.ShapeDtypeStruct((n_exp, d, ff2 // 2), BF16)
    return pl.pallas_call(
        functools.partial(_split_kernel, lanes=MXU_HALF),
        grid=(n_exp, ff2 // (2 * tn)),
        in_specs=[pl.BlockSpec((1, d, 2 * tn), lambda e, j: (e, 0, j))],
        out_specs=[pl.BlockSpec((1, d, tn), lambda e, j: (e, 0, j))] * 2,
        out_shape=[out, out],
        compiler_params=_params(("arbitrary", "arbitrary")),
        name="split_gate_up",
    )(w_gu)


def _proj_kernel(*refs, carry_mode, t_tiles, t_len, q_scale):
    x_ref, g_ref = refs[0:2]
    w_refs = refs[2:2 + N_SECTIONS]
    cw_ref = refs[2 + N_SECTIONS]
    if carry_mode:
        prev_ref = refs[3 + N_SECTIONS]
        outs = refs[4 + N_SECTIONS:]
    else:
        p0_ref, p1_ref = refs[3 + N_SECTIONS:5 + N_SECTIONS]
        outs = refs[5 + N_SECTIONS:]
    q_out, k_out, v_out, kb_out, vt_out, mconv_out, sga_out, st_out, h_scr, carry_scr = outs
    i = pl.program_id(0)
    j = pl.program_id(1)

    @pl.when(j == 0)
    def _():
        x = x_ref[...]
        ms = jnp.mean(x * x, axis=-1, keepdims=True)
        h_scr[...] = (x * lax.rsqrt(ms + EPS) * g_ref[...]).astype(BF16)

    h = h_scr[...]
    wq, wk, wv, wcb, wcc, wch, wga, wgb = w_refs
    q_out[...] = (_dot(h, wq[...]) * q_scale).astype(BF16)
    k = _dot(h, wk[...])
    k_out[...] = k
    kb_out[...] = k.astype(BF16)
    v = _dot(h, wv[...])
    v_out[...] = v
    vt_out[0, 0] = v.T.astype(BF16)
    sga_out[...] = jax.nn.sigmoid(_dot(h, wga[...])).astype(BF16)

    u = _dot(h, wcc[...]) * _dot(h, wch[...])
    tm = u.shape[0]
    r = lax.broadcasted_iota(I32, u.shape, 0)
    d1 = pltpu.roll(u, 1, 0)
    d2 = pltpu.roll(u, 2, 0)
    if carry_mode:
        @pl.when(i % t_tiles == 0)
        def _():
            carry_scr[j] = prev_ref[0]
        c = carry_scr[j]
        c0, c1 = c[0:1, :], c[1:2, :]
        s1 = jnp.where(r == 0, c1, d1)
        s2 = jnp.where(r == 0, c0, jnp.where(r == 1, c1, d2))
        carry_scr[j] = u[tm - 2:tm, :]
        st_out[0] = u[tm - 2:tm, :]
    else:
        t = r % t_len
        p0, p1 = p0_ref[...], p1_ref[...]
        s1 = jnp.where(t == 0, p1, d1)
        s2 = jnp.where(t == 0, p0, jnp.where(t == 1, p1, d2))
        st_out[...] = u
    cw = cw_ref[...]
    conv = s2 * cw[0:1, :] + s1 * cw[1:2, :] + u * cw[2:3, :]
    gate_b = _dot(h, wcb[...])
    mconv_out[...] = (jax.nn.sigmoid(_dot(h, wgb[...])) * gate_b * conv).astype(BF16)


def _rms_proj(x2d, norm_g, w_bf, conv_w, prev, *, t_len, head_dim):
    n, d = x2d.shape
    tm, tn = min(PROJ_TM, n), PROJ_TN
    assert n % tm == 0 and d % tn == 0 and t_len >= 2
    carry_mode = t_len % tm == 0
    n_col = d // tn
    row_blk = pl.BlockSpec((tm, tn), lambda i, j: (i, j))
    in_specs = [pl.BlockSpec((tm, d), lambda i, j: (i, 0)),
                pl.BlockSpec((1, d), lambda i, j: (0, 0))]
    in_specs += [pl.BlockSpec((d, tn), functools.partial(lambda i, j, s: (0, s * n_col + j), s=s))
                 for s in range(N_SECTIONS)]
    in_specs.append(pl.BlockSpec((3, tn), lambda i, j: (0, j)))
    if carry_mode:
        t_tiles = t_len // tm
        in_specs.append(pl.BlockSpec((1, 2, tn), lambda i, j: (i // t_tiles, 0, j)))
        prev_args = (prev,)
        st_shape = jax.ShapeDtypeStruct((n // tm, 2, d), F32)
        st_spec = pl.BlockSpec((1, 2, tn), lambda i, j: (i, 0, j))
    else:
        assert tm % t_len == 0
        t_tiles = 1
        in_specs += [row_blk, row_blk]
        prev_args = tuple(prev)
        st_shape = jax.ShapeDtypeStruct((n, d), F32)
        st_spec = row_blk
    row_bf = jax.ShapeDtypeStruct((n, d), BF16)
    row_f32 = jax.ShapeDtypeStruct((n, d), F32)
    out_shape = [row_bf, row_f32, row_f32, row_bf,
                 jax.ShapeDtypeStruct((n // tm, n_col, tn, tm), BF16), row_bf, row_bf, st_shape]
    vt_spec = pl.BlockSpec((1, 1, tn, tm), lambda i, j: (i, j, 0, 0))
    kern = functools.partial(_proj_kernel, carry_mode=carry_mode, t_tiles=t_tiles, t_len=t_len,
                             q_scale=head_dim ** -0.5 * LOG2E)
    return pl.pallas_call(
        kern,
        grid=(n // tm, n_col),
        in_specs=in_specs,
        out_specs=[row_blk] * 4 + [vt_spec, row_blk, row_blk, st_spec],
        out_shape=out_shape,
        scratch_shapes=[pltpu.VMEM((tm, d), BF16), pltpu.VMEM((n_col, 2, tn), F32)],
        compiler_params=_params(("arbitrary", "arbitrary")),
        name="rms_proj",
    )(x2d, norm_g.reshape(1, d), *([w_bf] * N_SECTIONS), conv_w, *prev_args)


def _t5_bias(table, dist):
    nb = table.shape[0]
    n = jnp.maximum(dist, 0)
    max_exact = nb // 2
    large = max_exact + (jnp.log(jnp.maximum(n, 1).astype(F32) / max_exact)
                         / math.log(MAX_DISTANCE / max_exact) * (nb - max_exact)).astype(I32)
    bucket = jnp.where(n < max_exact, n, jnp.minimum(large, nb - 1))
    per_bucket = (table.astype(F32) * LOG2E).reshape((nb, table.shape[1]) + (1,) * dist.ndim)
    bias = jnp.zeros((table.shape[1],) + dist.shape, F32)
    for b in range(nb):
        bias = jnp.where(bucket == b, per_bucket[b], bias)
    return jnp.where(dist >= 0, bias, MASKED)


def _pattn_kernel(lam_ref, cfar_ref, q_ref, k_ref, vt_ref, bd_ref, bs_ref, g_ref, o_ref, a_scr,
                  *, tb, hd, out_scale):
    h = pl.program_id(1)
    qi = pl.program_id(2)
    q = q_ref[...]
    qs = (q[:, :hd], q[:, hd:])
    cfar = cfar_ref[h]
    a_scr[...] = jnp.zeros(a_scr.shape, F32)

    def block(j, stats, bias):
        start = pl.multiple_of(j * tb, tb)
        kk = k_ref[pl.ds(start, tb), :]
        vt = vt_ref[j, 0]
        sts = [_dot_nt(kk[:, mi * hd:(mi + 1) * hd], qs[mi]) for mi in range(2)]
        new_stats, ps, alphas = [], [], []
        for mi in range(2):
            m_old, l_old = stats[mi]
            st = sts[mi]
            if bias is None:
                m_new = jnp.maximum(m_old, jnp.max(st, axis=0, keepdims=True) + cfar)
                p = jnp.exp2(st - (m_new - cfar))
            else:
                st = st + bias
                m_new = jnp.maximum(m_old, jnp.max(st, axis=0, keepdims=True))
                p = jnp.exp2(st - m_new)
            alpha = jnp.exp2(m_old - m_new)
            new_stats.append((m_new, alpha * l_old + jnp.sum(p, axis=0, keepdims=True)))
            ps.append(p.astype(BF16))
            alphas.append(alpha)
        for mi in range(2):
            a_scr[mi] = alphas[mi] * a_scr[mi] + _dot(vt, ps[mi])
        return tuple(new_stats)

    init = (jnp.full((1, tb), MASKED, F32), jnp.zeros((1, tb), F32))
    n_far = jnp.maximum(qi - 1, 0)
    stats = lax.fori_loop(0, n_far // 2,
                          lambda j, s: block(2 * j + 1, block(2 * j, s, None), None), (init, init))
    stats = lax.cond(n_far % 2 == 1, lambda s: block(n_far - 1, s, None), lambda s: s, stats)
    stats = lax.cond(qi >= 1, lambda s: block(qi - 1, s, bs_ref[0]), lambda s: s, stats)
    (_, l1), (_, l2) = block(qi, stats, bd_ref[0])

    ot = a_scr[0] * (1.0 / l1) - lam_ref[0] * (a_scr[1] * (1.0 / l2))
    ms = jnp.mean(ot * ot, axis=0, keepdims=True)
    ot = ot * lax.rsqrt(ms + EPS) * (g_ref[...] * out_scale)
    o_ref[...] = ot.T.astype(o_ref.dtype)


def _prompt_attn(q, kb, vt, table, lam, subln_g, *, b, t, n_heads, hd, lam_init):
    n, d = q.shape
    vd = 2 * hd
    tb = vt.shape[3]
    assert t % tb == 0 and tb + 1 >= MAX_DISTANCE and vt.shape == (n // tb, n_heads, vd, tb)
    nq = t // tb
    pos = jnp.arange(tb, dtype=I32)
    dist = pos[None, :] - pos[:, None]
    bias_diag = _t5_bias(table, dist)
    bias_sub = _t5_bias(table, tb + dist)
    cfar = table[table.shape[0] - 1].astype(F32) * LOG2E
    kern = functools.partial(_pattn_kernel, tb=tb, hd=hd, out_scale=1.0 - lam_init)
    smem = pl.BlockSpec(memory_space=pltpu.SMEM)
    return pl.pallas_call(
        kern,
        grid=(b, n_heads, nq),
        in_specs=[smem, smem,
                  pl.BlockSpec((tb, vd), lambda bi, h, i: (bi * nq + i, h)),
                  pl.BlockSpec((t, vd), lambda bi, h, i: (bi, h)),
                  pl.BlockSpec((nq, 1, vd, tb), lambda bi, h, i: (bi, h, 0, 0)),
                  pl.BlockSpec((1, tb, tb), lambda bi, h, i: (h, 0, 0)),
                  pl.BlockSpec((1, tb, tb), lambda bi, h, i: (h, 0, 0)),
                  pl.BlockSpec((vd, 1), lambda bi, h, i: (0, 0))],
        out_specs=pl.BlockSpec((tb, vd), lambda bi, h, i: (bi * nq + i, h)),
        out_shape=jax.ShapeDtypeStruct((n, d), BF16),
        scratch_shapes=[pltpu.VMEM((2, vd, tb), F32)],
        compiler_params=_params(("arbitrary", "arbitrary", "arbitrary")),
        name="prompt_attn",
    )(lam, cfar, q, kb, vt, bias_diag, bias_sub, subln_g.reshape(vd, 1))


def _decode_kernel(pt_ref, lam_ref, q_ref, *refs, n_pages, n_heads, tq, out_scale):
    k_refs = refs[:n_pages]
    v_refs = refs[n_pages:2 * n_pages]
    (bias_ref, kn_ref, vn_ref, bn_ref, g_ref, o_ref, m_scr, l_scr, a_scr) = refs[2 * n_pages:]
    c = pl.program_id(1)
    q = q_ref[0]
    page, _, vd = k_refs[0].shape[2:]
    keys = page * n_heads

    @pl.when(c == 0)
    def _():
        s = _dot_nt(q, kn_ref[0].astype(BF16)) + bn_ref[...]
        m = jnp.max(s, axis=-1, keepdims=True)
        p = jnp.exp2(s - m)
        m_scr[...] = m
        l_scr[...] = jnp.sum(p, axis=-1, keepdims=True)
        a_scr[...] = _dot(p.astype(BF16), vn_ref[0].astype(BF16))

    s = jnp.concatenate([_dot_nt(q, k_refs[i][0, 0].reshape(keys, vd).astype(BF16))
                         for i in range(n_pages)], axis=1) + bias_ref[0]
    m_old = m_scr[...]
    m_new = jnp.maximum(m_old, jnp.max(s, axis=-1, keepdims=True))
    p = jnp.exp2(s - m_new)
    alpha = jnp.exp2(m_old - m_new)
    pb = p.astype(BF16)
    pv = _dot(pb[:, 0:keys], v_refs[0][0, 0].reshape(keys, vd).astype(BF16))
    for i in range(1, n_pages):
        pv = pv + _dot(pb[:, i * keys:(i + 1) * keys],
                       v_refs[i][0, 0].reshape(keys, vd).astype(BF16))
    l_scr[...] = alpha * l_scr[...] + jnp.sum(p, axis=-1, keepdims=True)
    a_scr[...] = alpha * a_scr[...] + pv
    m_scr[...] = m_new

    @pl.when(c == pl.num_programs(1) - 1)
    def _():
        o_all = a_scr[...] / l_scr[...]
        for h in range(n_heads):
            r0 = h * 2 * tq
            o = o_all[r0:r0 + tq] - lam_ref[0] * o_all[r0 + tq:r0 + 2 * tq]
            ms = jnp.mean(o * o, axis=-1, keepdims=True)
            o_ref[0, :, h * vd:(h + 1) * vd] = o * lax.rsqrt(ms + EPS) * g_ref[...] * out_scale


def _decode_attn(q, k_new, v_new, cache_k, cache_v, layer, page_table, table, lam, subln_g,
                 *, db, tq, lam_init):
    _, _, page, n_heads, vd = cache_k.shape
    hd = vd // 2
    d = n_heads * vd
    np_seq = page_table.shape[1]
    cpp = DECODE_PAGES_PER_STEP
    assert np_seq % cpp == 0 and cpp * page + 1 >= MAX_DISTANCE
    n_chunks = np_seq // cpp
    past = np_seq * page
    rows = n_heads * 2 * tq
    new_pad = 128 // n_heads

    q5 = q.reshape(db, tq, n_heads, 2, hd).transpose(0, 2, 3, 1, 4)
    z = jnp.zeros_like(q5[:, :, 0])
    q_bd = jnp.concatenate([jnp.concatenate([q5[:, :, 0], z], -1),
                            jnp.concatenate([z, q5[:, :, 1]], -1)], axis=2).reshape(db, rows, vd)

    row_h = jnp.repeat(jnp.arange(n_heads, dtype=I32), 2 * tq)
    row_t = jnp.tile(jnp.arange(tq, dtype=I32), 2 * n_heads)

    def bias_for(key_pos, key_valid):
        per_head = _t5_bias(table, past + row_t[:, None] - key_pos[None, :])
        per_head = jnp.where(key_valid[None, None, :], per_head, MASKED)
        own = row_h[None, :, None] == jnp.arange(n_heads, dtype=I32)[:, None, None]
        return jnp.where(own, per_head, MASKED).transpose(1, 2, 0).reshape(rows, -1)

    span = cpp * page
    far_pos = jnp.arange(span, dtype=I32)
    near_pos = past - span + far_pos
    bias_past = jnp.stack([bias_for(far_pos, far_pos >= 0), bias_for(near_pos, far_pos >= 0)])
    new_t = jnp.arange(new_pad, dtype=I32)
    bias_new = bias_for(past + new_t, new_t < tq)
    pad = ((0, 0), (0, new_pad - tq), (0, 0))
    kn = jnp.pad(k_new.reshape(db, tq, d), pad).reshape(db, new_pad * n_heads, vd)
    vn = jnp.pad(v_new.reshape(db, tq, d), pad).reshape(db, new_pad * n_heads, vd)

    def page_map(i):
        return lambda b, c, pt: (layer, pt[b * np_seq + c * cpp + i], 0, 0, 0)

    page_specs = [pl.BlockSpec((1, 1, page, n_heads, vd), page_map(i)) for i in range(cpp)]
    kern = functools.partial(_decode_kernel, n_pages=cpp, n_heads=n_heads, tq=tq,
                             out_scale=1.0 - lam_init)
    grid_spec = pltpu.PrefetchScalarGridSpec(
        num_scalar_prefetch=1,
        grid=(db, n_chunks),
        in_specs=[pl.BlockSpec(memory_space=pltpu.SMEM),
                  pl.BlockSpec((1, rows, vd), lambda b, c, pt: (b, 0, 0))]
                 + page_specs + page_specs
                 + [pl.BlockSpec((1, rows, span * n_heads),
                                 lambda b, c, pt: (jnp.where(c == n_chunks - 1, 1, 0), 0, 0)),
                    pl.BlockSpec((1, new_pad * n_heads, vd), lambda b, c, pt: (b, 0, 0)),
                    pl.BlockSpec((1, new_pad * n_heads, vd), lambda b, c, pt: (b, 0, 0)),
                    pl.BlockSpec((rows, new_pad * n_heads), lambda b, c, pt: (0, 0)),
                    pl.BlockSpec((1, vd), lambda b, c, pt: (0, 0))],
        out_specs=pl.BlockSpec((1, tq, d), lambda b, c, pt: (b, 0, 0)),
        scratch_shapes=[pltpu.VMEM((rows, 1), F32), pltpu.VMEM((rows, 1), F32),
                        pltpu.VMEM((rows, vd), F32)],
    )
    out = pl.pallas_call(
        kern,
        grid_spec=grid_spec,
        out_shape=jax.ShapeDtypeStruct((db, tq, d), F32),
        compiler_params=_params(("arbitrary", "arbitrary")),
        name="decode_attn",
    )(page_table.reshape(-1), lam, q_bd, *([cache_k] * cpp), *([cache_v] * cpp),
      bias_past, kn, vn, bias_new, subln_g.reshape(1, vd))
    return out.reshape(db * tq, d).astype(BF16)


def _merge_kernel(attn_ref, sga_ref, mconv_ref, x_ref, wo_ref, g_ref, rwh_ref, rwl_ref, rb_ref,
                  cnt_in_ref, x1_ref, h2_ref, idx_ref, gate_ref, rank_ref, cnt_out_ref, carry_scr):
    i = pl.program_id(0)

    @pl.when(i == 0)
    def _():
        carry_scr[...] = cnt_in_ref[...]

    m = sga_ref[...].astype(F32) * attn_ref[...].astype(F32) + mconv_ref[...].astype(F32)
    x1 = x_ref[...] + _dot(m.astype(BF16), wo_ref[...])
    x1_ref[...] = x1
    ms = jnp.mean(x1 * x1, axis=-1, keepdims=True)
    h2 = x1 * lax.rsqrt(ms + EPS) * g_ref[...]
    h2_ref[...] = h2

    hh = h2.astype(BF16)
    hl = (h2 - hh.astype(F32)).astype(BF16)
    logits = (_dot_nt(rwh_ref[...], hh) + _dot_nt(rwh_ref[...], hl) + _dot_nt(rwl_ref[...], hh)
              + rb_ref[...])
    n_exp, tm = logits.shape
    e_iota = lax.broadcasted_iota(I32, (n_exp, tm), 0).astype(F32)
    work = logits
    tops, sels = [], []
    for k in range(TOP_K):
        mx = jnp.max(work, axis=0, keepdims=True)
        ik = jnp.min(jnp.where(work == mx, e_iota, float(n_exp)), axis=0, keepdims=True)
        sel = e_iota == ik
        work = jnp.where(sel, -jnp.inf, work)
        tops.append(mx)
        sels.append(sel)
        idx_ref[k:k + 1, :] = ik.astype(I32)
    exps = [jnp.exp(t - tops[0]) for t in tops]
    den = exps[0] + exps[1] + exps[2] + exps[3]
    for k in range(TOP_K):
        gate_ref[k:k + 1, :] = exps[k] / den

    assigned = sels[0] | sels[1] | sels[2] | sels[3]
    a_mat = jnp.where(assigned, 1.0, 0.0).astype(BF16)
    rr = lax.broadcasted_iota(I32, (tm, tm), 0)
    cc = lax.broadcasted_iota(I32, (tm, tm), 1)
    upper = jnp.where(rr < cc, 1.0, 0.0).astype(BF16)
    before = _dot(a_mat, upper) + carry_scr[:, 0:1]
    for k in range(TOP_K):
        rank_ref[k:k + 1, :] = jnp.sum(jnp.where(sels[k], before, 0.0), axis=0,
                                       keepdims=True).astype(I32)
    carry_scr[...] = carry_scr[...] + jnp.sum(a_mat.astype(F32), axis=1, keepdims=True)
    cnt_out_ref[...] = carry_scr[...]


def _merge_route(attn, sga, mconv, x2d, wo_bf, ffn_g, rw_hi, rw_lo, rb, cnt_in):
    n, d = x2d.shape
    tm = min(MERGE_TM, n)
    assert n % tm == 0
    n_exp = rw_hi.shape[0]
    row = pl.BlockSpec((tm, d), lambda i: (i, 0))
    full = lambda shape: pl.BlockSpec(shape, lambda i: (0,) * len(shape))
    tok = pl.BlockSpec((TOP_K, tm), lambda i: (0, i))
    return pl.pallas_call(
        _merge_kernel,
        grid=(n // tm,),
        in_specs=[row, row, row, row, full((d, d)), full((1, d)), full((n_exp, d)),
                  full((n_exp, d)), full((n_exp, 1)), full((n_exp, 128))],
        out_specs=[row, row, tok, tok, tok, full((n_exp, 128))],
        out_shape=[jax.ShapeDtypeStruct((n, d), F32), jax.ShapeDtypeStruct((n, d), F32),
                   jax.ShapeDtypeStruct((TOP_K, n), I32), jax.ShapeDtypeStruct((TOP_K, n), F32),
                   jax.ShapeDtypeStruct((TOP_K, n), I32), jax.ShapeDtypeStruct((n_exp, 128), F32)],
        scratch_shapes=[pltpu.VMEM((n_exp, 128), F32)],
        compiler_params=_params(("arbitrary",)),
        name="merge_route",
    )(attn, sga, mconv, x2d, wo_bf, ffn_g.reshape(1, d), rw_hi, rw_lo, rb, cnt_in)


def _tile_indices(dest, tt):
    k, n = dest.shape
    return dest.reshape(k, n // tt, tt).transpose(1, 0, 2).reshape(n // tt, k * tt)


def _dispatch_kernel(dest_ref, h_ref, xs_in_ref, xs_ref, idx_smem, idx_sem, row_sem, *, tt):
    del xs_in_ref
    i = pl.program_id(0)
    cp = pltpu.make_async_copy(dest_ref.at[i], idx_smem, idx_sem)
    cp.start()
    cp.wait()

    def row_copy(r, k):
        return pltpu.make_async_copy(h_ref.at[pl.ds(r, 1)],
                                     xs_ref.at[pl.ds(idx_smem[k * tt + r], 1)], row_sem)

    def issue(r, carry):
        for k in range(TOP_K):
            row_copy(r, k).start()
        return carry

    def drain(r, carry):
        for k in range(TOP_K):
            row_copy(r, k).wait()
        return carry

    lax.fori_loop(0, tt, issue, 0)
    lax.fori_loop(0, tt, drain, 0)


def _dispatch(h2, dest, xs):
    n, d = h2.shape
    tt = min(DISPATCH_TT, n)
    assert n % tt == 0
    tiles = _tile_indices(dest, tt)
    return pl.pallas_call(
        functools.partial(_dispatch_kernel, tt=tt),
        grid=(n // tt,),
        in_specs=[pl.BlockSpec(tiles.shape, lambda i: (0, 0)),
                  pl.BlockSpec((tt, d), lambda i: (i, 0)), pl.BlockSpec(memory_space=pl.ANY)],
        out_specs=pl.BlockSpec(memory_space=pl.ANY),
        out_shape=jax.ShapeDtypeStruct(xs.shape, xs.dtype),
        scratch_shapes=[pltpu.SMEM((TOP_K * tt,), I32), pltpu.SemaphoreType.DMA(()),
                        pltpu.SemaphoreType.DMA(())],
        input_output_aliases={2: 0},
        compiler_params=_params(("arbitrary",)),
        name="dispatch",
    )(tiles, h2, xs)


def _moe_kernel(be_ref, nu_ref, xs_ref, wg_ref, wu_ref, wd_ref, bg_ref, bu_ref, bd_ref, y_ref,
                xb_scr):
    del be_ref
    i = pl.program_id(0)
    f = pl.program_id(1)

    @pl.when(i < nu_ref[0])
    def _():
        @pl.when(f == 0)
        def _():
            xb_scr[...] = xs_ref[...].astype(BF16)

        xb = xb_scr[...]
        gate = jnp.minimum(_dot(xb, wg_ref[0]) + bg_ref[0], SWIGLU_LIMIT)
        up = jnp.clip(_dot(xb, wu_ref[0]) + bu_ref[0], -SWIGLU_LIMIT, SWIGLU_LIMIT)
        act = (up + 1.0) * gate * jax.nn.sigmoid(SWIGLU_ALPHA * gate)
        part = _dot(act.astype(BF16), wd_ref[0])

        @pl.when(f == 0)
        def _():
            y_ref[...] = part + bd_ref[0]

        @pl.when(f > 0)
        def _():
            y_ref[...] += part

    @pl.when((i >= nu_ref[0]) & (f == 0))
    def _():
        y_ref[...] = jnp.zeros(y_ref.shape, y_ref.dtype)


def _moe_ffn(xs, block_e, n_used, wg, wu, wd, bg, bu, bd):
    rows, d = xs.shape
    n_exp, _, ff = wg.shape
    tm, fc = MOE_TM, MOE_FC
    assert rows % tm == 0 and ff % fc == 0
    nf = ff // fc

    def blk(i, nu):
        return jnp.minimum(i, nu[0] - 1)

    def fch(i, f, nu):
        return jnp.where(i < nu[0], f, nf - 1)

    grid_spec = pltpu.PrefetchScalarGridSpec(
        num_scalar_prefetch=2,
        grid=(rows // tm, nf),
        in_specs=[pl.BlockSpec((tm, d), lambda i, f, be, nu: (blk(i, nu), 0)),
                  pl.BlockSpec((1, d, fc), lambda i, f, be, nu: (be[blk(i, nu)], 0, fch(i, f, nu))),
                  pl.BlockSpec((1, d, fc), lambda i, f, be, nu: (be[blk(i, nu)], 0, fch(i, f, nu))),
                  pl.BlockSpec((1, fc, d), lambda i, f, be, nu: (be[blk(i, nu)], fch(i, f, nu), 0)),
                  pl.BlockSpec((1, 1, fc), lambda i, f, be, nu: (be[blk(i, nu)], 0, fch(i, f, nu))),
                  pl.BlockSpec((1, 1, fc), lambda i, f, be, nu: (be[blk(i, nu)], 0, fch(i, f, nu))),
                  pl.BlockSpec((1, 1, d), lambda i, f, be, nu: (be[blk(i, nu)], 0, 0))],
        out_specs=pl.BlockSpec((tm, d), lambda i, f, be, nu: (i, 0)),
        scratch_shapes=[pltpu.VMEM((tm, d), BF16)],
    )
    return pl.pallas_call(
        _moe_kernel,
        grid_spec=grid_spec,
        out_shape=jax.ShapeDtypeStruct((rows, d), F32),
        compiler_params=_params(("arbitrary", "arbitrary"), MOE_VMEM_LIMIT),
        name="moe_ffn",
    )(block_e, n_used, xs, wg, wu, wd, bg, bu, bd)


def _combine_kernel(dest_ref, x1_ref, gate_ref, g_ref, yb_ref, y_ref, buf, idx_smem, idx_sem,
                    row_sem, *, tc):
    i = pl.program_id(0)
    cp = pltpu.make_async_copy(dest_ref.at[i], idx_smem, idx_sem)
    cp.start()
    cp.wait()

    def row_copy(r, k):
        return pltpu.make_async_copy(yb_ref.at[pl.ds(idx_smem[k * tc + r], 1)],
                                     buf.at[k, pl.ds(r, 1)], row_sem)

    def issue(r, carry):
        for k in range(TOP_K):
            row_copy(r, k).start()
        return carry

    def drain(r, carry):
        for k in range(TOP_K):
            row_copy(r, k).wait()
        return carry

    lax.fori_loop(0, tc, issue, 0)
    lax.fori_loop(0, tc, drain, 0)

    gates = gate_ref[...]
    y = x1_ref[...]
    for k in range(TOP_K):
        y = y + gates[:, k:k + 1] * buf[k]
    ms = jnp.mean(y * y, axis=-1, keepdims=True)
    y_ref[...] = y * lax.rsqrt(ms + EPS) * g_ref[...]


def _combine(x1, gates, dest, yb, final_g):
    n, d = x1.shape
    tc = min(COMBINE_TC, n)
    assert n % tc == 0
    tiles = _tile_indices(dest, tc)
    return pl.pallas_call(
        functools.partial(_combine_kernel, tc=tc),
        grid=(n // tc,),
        in_specs=[pl.BlockSpec(tiles.shape, lambda i: (0, 0)),
                  pl.BlockSpec((tc, d), lambda i: (i, 0)),
                  pl.BlockSpec((tc, TOP_K), lambda i: (i, 0)),
                  pl.BlockSpec((1, d), lambda i: (0, 0)),
                  pl.BlockSpec(memory_space=pl.ANY)],
        out_specs=pl.BlockSpec((tc, d), lambda i: (i, 0)),
        out_shape=jax.ShapeDtypeStruct((n, d), F32),
        scratch_shapes=[pltpu.VMEM((TOP_K, tc, d), F32), pltpu.SMEM((TOP_K * tc,), I32),
                        pltpu.SemaphoreType.DMA(()), pltpu.SemaphoreType.DMA(())],
        compiler_params=_params(("arbitrary",)),
        name="combine",
    )(tiles, x1, gates.T, final_g.reshape(1, d), yb)


def _expert_offsets(idx, rank, pad_start):
    dest = rank
    for e in range(pad_start.shape[0]):
        dest = dest + jnp.where(idx == e, pad_start[e], 0)
    return dest


def kernel(x_prompt, x_sample, cache_k, cache_v, state_conv, page_table, attn_norm_g, w_in,
           lambda_q1, lambda_k1, lambda_q2, lambda_k2, subln_g, rel_bias_table, conv_w, w_out,
           ffn_norm_g, router_w, router_b, w_gate_up, b_gate_up, w_down, b_down, final_norm_g):
    bp, tp, d = x_prompt.shape
    bs, ts, _ = x_sample.shape
    depth, _, _, n_heads, vd = cache_v.shape
    hd = vd // 2
    n_exp = router_w.shape[-1]
    ff = w_down.shape[2]
    np_, ns_ = bp * tp, bs * ts
    assert depth == 1, "only a one-layer stack is supported"
    l = 0
    lam_init = 0.8 - 0.6 * math.exp(-0.3 * l)
    lam = (jnp.exp(jnp.sum(lambda_q1[l] * lambda_k1[l]).astype(F32))
           - jnp.exp(jnp.sum(lambda_q2[l] * lambda_k2[l]).astype(F32)) + lam_init).reshape(1)
    w_bf = _cast_bf16(w_in[l])
    wo_bf = _cast_bf16(w_out[l])
    xp = x_prompt.reshape(np_, d)
    xs_ = x_sample.reshape(ns_, d)

    qp, kp, vp, kpb, vpt, mconv_p, sga_p, st_p = _rms_proj(
        xp, attn_norm_g[l], w_bf, conv_w[l], jnp.zeros((bp, 2, d), F32), t_len=tp, head_dim=hd)
    attn_p = _prompt_attn(qp, kpb, vpt, rel_bias_table, lam, subln_g[l], b=bp, t=tp,
                          n_heads=n_heads, hd=hd, lam_init=lam_init)
    st_prev = state_conv[l]
    prev_rows = (jnp.repeat(st_prev[:, 0], ts, axis=0), jnp.repeat(st_prev[:, 1], ts, axis=0))
    qs, ks, vs, _, _, mconv_s, sga_s, u_s = _rms_proj(
        xs_, attn_norm_g[l], w_bf, conv_w[l], prev_rows, t_len=ts, head_dim=hd)
    attn_s = _decode_attn(qs, ks, vs, cache_k, cache_v, l, page_table, rel_bias_table, lam,
                          subln_g[l], db=bs, tq=ts, lam_init=lam_init)

    rw_t = router_w[l].T
    rw_hi = rw_t.astype(BF16)
    rw_lo = (rw_t - rw_hi.astype(F32)).astype(BF16)
    rb = router_b[l].astype(F32).reshape(n_exp, 1)
    cnt0 = jnp.zeros((n_exp, 128), F32)
    x1p, h2p, idx_p, gate_p, rank_p, cnt1 = _merge_route(
        attn_p, sga_p, mconv_p, xp, wo_bf, ffn_norm_g[l], rw_hi, rw_lo, rb, cnt0)
    x1s, h2s, idx_s, gate_s, rank_s, cnt2 = _merge_route(
        attn_s, sga_s, mconv_s, xs_, wo_bf, ffn_norm_g[l], rw_hi, rw_lo, rb, cnt1)

    counts = cnt2[:, 0].astype(I32)
    padded = (counts + MOE_TM - 1) // MOE_TM * MOE_TM
    pad_end = jnp.cumsum(padded)
    pad_start = pad_end - padded
    n_blocks = -(-((np_ + ns_) * TOP_K) // MOE_TM) + n_exp
    block_start = jnp.arange(n_blocks, dtype=I32) * MOE_TM
    block_e = jnp.minimum(jnp.sum(block_start[:, None] >= pad_end[None, :], axis=1),
                          n_exp - 1).astype(I32)
    n_used = (pad_end[-1] // MOE_TM).astype(I32).reshape(1)
    dest_p = _expert_offsets(idx_p, rank_p, pad_start)
    dest_s = _expert_offsets(idx_s, rank_s, pad_start)

    xsorted = jnp.zeros((n_blocks * MOE_TM, d), F32)
    xsorted = _dispatch(h2p, dest_p, xsorted)
    xsorted = _dispatch(h2s, dest_s, xsorted)

    wg, wu = _split_gate_up(w_gate_up[l])
    wd = _cast_bf16(w_down[l].reshape(n_exp * ff, d)).reshape(n_exp, ff, d)
    b_gu = b_gate_up[l].reshape(n_exp, ff, 2)
    yb = _moe_ffn(xsorted, block_e, n_used, wg, wu, wd,
                  b_gu[:, :, 0].reshape(n_exp, 1, ff), b_gu[:, :, 1].reshape(n_exp, 1, ff),
                  b_down[l].reshape(n_exp, 1, d))

    y_p = _combine(x1p, gate_p, dest_p, yb, final_norm_g)
    y_s = _combine(x1s, gate_s, dest_s, yb, final_norm_g)

    def stack(a, b, t):
        return a.reshape(depth, b, t, n_heads, vd)

    return (y_p.reshape(bp, tp, d), y_s.reshape(bs, ts, d),
            stack(kp, bp, tp), stack(vp, bp, tp), stack(ks, bs, ts), stack(vs, bs, ts),
            st_p.reshape(bp, -1, 2, d)[None, :, -1], u_s.reshape(bs, ts, d)[None, :, ts - 2:])
```

```python
import functools
import math

import jax
import jax.numpy as jnp
from jax import lax
from jax.experimental import pallas as pl
from jax.experimental.pallas import tpu as pltpu

F32 = jnp.float32
BF16 = jnp.bfloat16
I32 = jnp.int32

EPS = 1e-6
MAX_DISTANCE = 128
TOP_K = 4
SWIGLU_LIMIT = 7.0
SWIGLU_ALPHA = 1.702
MASKED = -1e30
LOG2E = math.log2(math.e)
N_SECTIONS = 8

VMEM_LIMIT = 48 * 1024 * 1024
MXU_HALF = 128
PROJ_TM, PROJ_TN = 512, 256
FAR_UNROLL = 2
DECODE_PAGES_PER_STEP = 4
DECODE_VMEM_LIMIT = 56 * 1024 * 1024
MERGE_TM = 512
MOE_TM, MOE_FC = 512, 1024
MOE_VMEM_LIMIT = 56 * 1024 * 1024
DISPATCH_TT = 512
COMBINE_TC = 128
CAST_BLOCK_BYTES = 4 * 1024 * 1024
ATTN_VMEM_LIMIT = 56 * 1024 * 1024


def _params(sem, vmem_limit=VMEM_LIMIT):
    return pltpu.CompilerParams(dimension_semantics=sem, vmem_limit_bytes=vmem_limit)


def _dot(a, b):
    return jnp.dot(a, b, preferred_element_type=F32)


def _dot_nt(a, b):
    return lax.dot_general(a, b, (((1,), (1,)), ((), ())), preferred_element_type=F32)


def _cast_kernel(x_ref, o_ref):
    o_ref[...] = x_ref[...].astype(o_ref.dtype)


def _cast_bf16(x2d):
    r, c = x2d.shape
    tr = min(CAST_BLOCK_BYTES // (4 * c), r)
    assert r % tr == 0 and tr % 16 == 0
    return pl.pallas_call(
        _cast_kernel,
        grid=(r // tr,),
        in_specs=[pl.BlockSpec((tr, c), lambda i: (i, 0))],
        out_specs=pl.BlockSpec((tr, c), lambda i: (i, 0)),
        out_shape=jax.ShapeDtypeStruct((r, c), BF16),
        compiler_params=_params(("arbitrary",)),
        name="cast_bf16",
    )(x2d)


def _proj_kernel(*refs, carry_mode, t_tiles, t_len, q_scale):
    x_ref, g_ref = refs[0:2]
    w_refs = refs[2:2 + N_SECTIONS]
    cw_ref = refs[2 + N_SECTIONS]
    if carry_mode:
        prev_ref = refs[3 + N_SECTIONS]
        outs = refs[4 + N_SECTIONS:]
    else:
        p0_ref, p1_ref = refs[3 + N_SECTIONS:5 + N_SECTIONS]
        outs = refs[5 + N_SECTIONS:]
    q_out, k_out, v_out, kb_out, vt_out, mconv_out, sga_out, st_out, h_scr, carry_scr = outs
    i = pl.program_id(0)
    j = pl.program_id(1)

    @pl.when(j == 0)
    def _():
        x = x_ref[...]
        ms = jnp.mean(x * x, axis=-1, keepdims=True)
        h_scr[...] = (x * lax.rsqrt(ms + EPS) * g_ref[...]).astype(BF16)

    h = h_scr[...]
    wq, wk, wv, wcb, wcc, wch, wga, wgb = w_refs
    q_out[...] = (_dot(h, wq[...]) * q_scale).astype(BF16)
    k = _dot(h, wk[...])
    k_out[...] = k
    kb_out[...] = k.astype(BF16)
    v = _dot(h, wv[...])
    v_out[...] = v
    vt_out[0, 0] = v.T.astype(BF16)
    sga_out[...] = jax.nn.sigmoid(_dot(h, wga[...])).astype(BF16)

    u = _dot(h, wcc[...]) * _dot(h, wch[...])
    tm = u.shape[0]
    r = lax.broadcasted_iota(I32, u.shape, 0)
    d1 = pltpu.roll(u, 1, 0)
    d2 = pltpu.roll(u, 2, 0)
    if carry_mode:
        @pl.when(i % t_tiles == 0)
        def _():
            carry_scr[j] = prev_ref[0]
        c = carry_scr[j]
        c0, c1 = c[0:1, :], c[1:2, :]
        s1 = jnp.where(r == 0, c1, d1)
        s2 = jnp.where(r == 0, c0, jnp.where(r == 1, c1, d2))
        carry_scr[j] = u[tm - 2:tm, :]
        st_out[0] = u[tm - 2:tm, :]
    else:
        t = r % t_len
        p0, p1 = p0_ref[...], p1_ref[...]
        s1 = jnp.where(t == 0, p1, d1)
        s2 = jnp.where(t == 0, p0, jnp.where(t == 1, p1, d2))
        st_out[...] = u
    cw = cw_ref[...]
    conv = s2 * cw[0:1, :] + s1 * cw[1:2, :] + u * cw[2:3, :]
    gate_b = _dot(h, wcb[...])
    mconv_out[...] = (jax.nn.sigmoid(_dot(h, wgb[...])) * gate_b * conv).astype(BF16)


def _rms_proj(x2d, norm_g, w_bf, conv_w, prev, *, t_len, head_dim):
    n, d = x2d.shape
    tm, tn = min(PROJ_TM, n), PROJ_TN
    assert n % tm == 0 and d % tn == 0 and t_len >= 2
    carry_mode = t_len % tm == 0
    n_col = d // tn
    row_blk = pl.BlockSpec((tm, tn), lambda i, j: (i, j))
    in_specs = [pl.BlockSpec((tm, d), lambda i, j: (i, 0)),
                pl.BlockSpec((1, d), lambda i, j: (0, 0))]
    in_specs += [pl.BlockSpec((d, tn), functools.partial(lambda i, j, s: (0, s * n_col + j), s=s))
                 for s in range(N_SECTIONS)]
    in_specs.append(pl.BlockSpec((3, tn), lambda i, j: (0, j)))
    if carry_mode:
        t_tiles = t_len // tm
        in_specs.append(pl.BlockSpec((1, 2, tn), lambda i, j: (i // t_tiles, 0, j)))
        prev_args = (prev,)
        st_shape = jax.ShapeDtypeStruct((n // tm, 2, d), F32)
        st_spec = pl.BlockSpec((1, 2, tn), lambda i, j: (i, 0, j))
    else:
        assert tm % t_len == 0
        t_tiles = 1
        in_specs += [row_blk, row_blk]
        prev_args = tuple(prev)
        st_shape = jax.ShapeDtypeStruct((n, d), F32)
        st_spec = row_blk
    row_bf = jax.ShapeDtypeStruct((n, d), BF16)
    row_f32 = jax.ShapeDtypeStruct((n, d), F32)
    out_shape = [row_bf, row_f32, row_f32, row_bf,
                 jax.ShapeDtypeStruct((n // tm, n_col, tn, tm), BF16), row_bf, row_bf, st_shape]
    vt_spec = pl.BlockSpec((1, 1, tn, tm), lambda i, j: (i, j, 0, 0))
    kern = functools.partial(_proj_kernel, carry_mode=carry_mode, t_tiles=t_tiles, t_len=t_len,
                             q_scale=head_dim ** -0.5 * LOG2E)
    return pl.pallas_call(
        kern,
        grid=(n // tm, n_col),
        in_specs=in_specs,
        out_specs=[row_blk] * 4 + [vt_spec, row_blk, row_blk, st_spec],
        out_shape=out_shape,
        scratch_shapes=[pltpu.VMEM((tm, d), BF16), pltpu.VMEM((n_col, 2, tn), F32)],
        compiler_params=_params(("arbitrary", "arbitrary")),
        name="rms_proj",
    )(x2d, norm_g.reshape(1, d), *([w_bf] * N_SECTIONS), conv_w, *prev_args)


def _t5_bias(table, dist):
    nb = table.shape[0]
    n = jnp.maximum(dist, 0)
    max_exact = nb // 2
    large = max_exact + (jnp.log(jnp.maximum(n, 1).astype(F32) / max_exact)
                         / math.log(MAX_DISTANCE / max_exact) * (nb - max_exact)).astype(I32)
    bucket = jnp.where(n < max_exact, n, jnp.minimum(large, nb - 1))
    per_bucket = (table.astype(F32) * LOG2E).reshape((nb, table.shape[1]) + (1,) * dist.ndim)
    buckets = jnp.arange(nb, dtype=I32).reshape((nb, 1) + (1,) * dist.ndim)
    bias = jnp.sum(jnp.where(bucket[None, None] == buckets, per_bucket, 0.0), axis=0)
    return jnp.where(dist >= 0, bias, MASKED)


def _split_gate_up_block(w_ref, g_ref, u_ref):
    lanes = MXU_HALF
    width = 2 * lanes
    w = w_ref[...].astype(BF16)
    rr = lax.broadcasted_iota(I32, (width, width), 0)
    cc = lax.broadcasted_iota(I32, (width, width), 1)
    src = jnp.where(cc < lanes, 2 * cc, 2 * (cc - lanes) + 1)
    perm = jnp.where(rr == src, 1.0, 0.0).astype(BF16)
    for b in range(w.shape[1] // width):
        moved = _dot(w[:, b * width:(b + 1) * width], perm).astype(BF16)
        g_ref[:, b * lanes:(b + 1) * lanes] = moved[:, :lanes]
        u_ref[:, b * lanes:(b + 1) * lanes] = moved[:, lanes:]


def _pattn_kernel(lam_ref, cfar_ref, q_ref, k_ref, vt_ref, bd_ref, bs_ref, g_ref, wgu_ref, wdn_ref,
                  o_ref, wg_out, wu_out, wd_out, a_scr, *, tb, hd, out_scale):
    h = pl.program_id(1)
    qi = pl.program_id(2)
    _split_gate_up_block(wgu_ref, wg_out, wu_out)
    wd_out[...] = wdn_ref[...].astype(BF16)

    q = q_ref[...]
    qs = (q[:, :hd], q[:, hd:])
    cfar = cfar_ref[h]
    a_scr[...] = jnp.zeros(a_scr.shape, F32)

    def block(j, stats, bias):
        start = pl.multiple_of(j * tb, tb)
        kk = k_ref[pl.ds(start, tb), :]
        vt = vt_ref[j, 0]
        sts = [_dot_nt(kk[:, mi * hd:(mi + 1) * hd], qs[mi]) for mi in range(2)]
        new_stats, ps, alphas = [], [], []
        for mi in range(2):
            m_old, l_old = stats[mi]
            st = sts[mi]
            if bias is None:
                m_new = jnp.maximum(m_old, jnp.max(st, axis=0, keepdims=True) + cfar)
                p = jnp.exp2(st - (m_new - cfar))
            else:
                st = st + bias
                m_new = jnp.maximum(m_old, jnp.max(st, axis=0, keepdims=True))
                p = jnp.exp2(st - m_new)
            alpha = jnp.exp2(m_old - m_new)
            new_stats.append((m_new, alpha * l_old + jnp.sum(p, axis=0, keepdims=True)))
            ps.append(p.astype(BF16))
            alphas.append(alpha)
        for mi in range(2):
            a_scr[mi] = alphas[mi] * a_scr[mi] + _dot(vt, ps[mi])
        return tuple(new_stats)

    init = (jnp.full((1, tb), MASKED, F32), jnp.zeros((1, tb), F32))
    n_far = jnp.maximum(qi - 1, 0)

    def far_group(j, s):
        for u in range(FAR_UNROLL):
            s = block(FAR_UNROLL * j + u, s, None)
        return s

    n_group = n_far // FAR_UNROLL
    stats = lax.fori_loop(0, n_group, far_group, (init, init))
    stats = lax.fori_loop(n_group * FAR_UNROLL, n_far, lambda j, s: block(j, s, None), stats)
    stats = lax.cond(qi >= 1, lambda s: block(qi - 1, s, bs_ref[0]), lambda s: s, stats)
    (_, l1), (_, l2) = block(qi, stats, bd_ref[0])

    ot = a_scr[0] * (1.0 / l1) - lam_ref[0] * (a_scr[1] * (1.0 / l2))
    ms = jnp.mean(ot * ot, axis=0, keepdims=True)
    ot = ot * lax.rsqrt(ms + EPS) * (g_ref[...] * out_scale)
    o_ref[...] = ot.T.astype(o_ref.dtype)


def _prompt_attn(q, kb, vt, table, lam, subln_g, w_gu, w_dn, *, b, t, n_heads, hd, lam_init):
    n, d = q.shape
    vd = 2 * hd
    tb = vt.shape[3]
    assert t % tb == 0 and tb + 1 >= MAX_DISTANCE and vt.shape == (n // tb, n_heads, vd, tb)
    nq = t // tb
    steps = b * n_heads * nq
    gu_rows, dn_rows = w_gu.shape[0] // steps, w_dn.shape[0] // steps
    assert gu_rows * steps == w_gu.shape[0] and dn_rows * steps == w_dn.shape[0]
    assert gu_rows % 16 == 0 and dn_rows % 16 == 0 and w_gu.shape[1] % (4 * MXU_HALF) == 0
    ff = w_gu.shape[1] // 2

    def slab(rows, cols):
        return pl.BlockSpec((rows, cols), lambda bi, h, i: ((bi * n_heads + h) * nq + i, 0))

    pos = jnp.arange(tb, dtype=I32)
    dist = pos[None, :] - pos[:, None]
    bias_diag = _t5_bias(table, dist)
    bias_sub = _t5_bias(table, tb + dist)
    cfar = table[table.shape[0] - 1].astype(F32) * LOG2E
    kern = functools.partial(_pattn_kernel, tb=tb, hd=hd, out_scale=1.0 - lam_init)
    smem = pl.BlockSpec(memory_space=pltpu.SMEM)
    return pl.pallas_call(
        kern,
        grid=(b, n_heads, nq),
        in_specs=[smem, smem,
                  pl.BlockSpec((tb, vd), lambda bi, h, i: (bi * nq + i, h)),
                  pl.BlockSpec((t, vd), lambda bi, h, i: (bi, h)),
                  pl.BlockSpec((nq, 1, vd, tb), lambda bi, h, i: (bi, h, 0, 0)),
                  pl.BlockSpec((1, tb, tb), lambda bi, h, i: (h, 0, 0)),
                  pl.BlockSpec((1, tb, tb), lambda bi, h, i: (h, 0, 0)),
                  pl.BlockSpec((vd, 1), lambda bi, h, i: (0, 0)),
                  slab(gu_rows, 2 * ff), slab(dn_rows, w_dn.shape[1])],
        out_specs=[pl.BlockSpec((tb, vd), lambda bi, h, i: (bi * nq + i, h)),
                   slab(gu_rows, ff), slab(gu_rows, ff), slab(dn_rows, w_dn.shape[1])],
        out_shape=[jax.ShapeDtypeStruct((n, d), BF16),
                   jax.ShapeDtypeStruct((w_gu.shape[0], ff), BF16),
                   jax.ShapeDtypeStruct((w_gu.shape[0], ff), BF16),
                   jax.ShapeDtypeStruct(w_dn.shape, BF16)],
        scratch_shapes=[pltpu.VMEM((2, vd, tb), F32)],
        compiler_params=_params(("arbitrary", "arbitrary", "arbitrary"), ATTN_VMEM_LIMIT),
        name="prompt_attn",
    )(lam, cfar, q, kb, vt, bias_diag, bias_sub, subln_g.reshape(vd, 1), w_gu, w_dn)


def _decode_kernel(pt_ref, lam_ref, q_ref, *refs, n_pages, n_heads, tq, out_scale):
    k_refs = refs[:n_pages]
    v_refs = refs[n_pages:2 * n_pages]
    (bias_ref, kn_ref, vn_ref, bn_ref, g_ref, o_ref, m_scr, l_scr, a_scr) = refs[2 * n_pages:]
    c = pl.program_id(1)
    q = q_ref[0]
    page, _, vd = k_refs[0].shape[2:]
    keys = page * n_heads

    @pl.when(c == 0)
    def _():
        s = _dot_nt(q, kn_ref[0].astype(BF16)) + bn_ref[...]
        m = jnp.max(s, axis=-1, keepdims=True)
        p = jnp.exp2(s - m)
        m_scr[...] = m
        l_scr[...] = jnp.sum(p, axis=-1, keepdims=True)
        a_scr[...] = _dot(p.astype(BF16), vn_ref[0].astype(BF16))

    s = jnp.concatenate([_dot_nt(q, k_refs[i][0, 0].reshape(keys, vd).astype(BF16))
                         for i in range(n_pages)], axis=1) + bias_ref[0]
    m_old = m_scr[...]
    m_new = jnp.maximum(m_old, jnp.max(s, axis=-1, keepdims=True))
    p = jnp.exp2(s - m_new)
    alpha = jnp.exp2(m_old - m_new)
    pb = p.astype(BF16)
    pv = _dot(pb[:, 0:keys], v_refs[0][0, 0].reshape(keys, vd).astype(BF16))
    for i in range(1, n_pages):
        pv = pv + _dot(pb[:, i * keys:(i + 1) * keys],
                       v_refs[i][0, 0].reshape(keys, vd).astype(BF16))
    l_scr[...] = alpha * l_scr[...] + jnp.sum(p, axis=-1, keepdims=True)
    a_scr[...] = alpha * a_scr[...] + pv
    m_scr[...] = m_new

    @pl.when(c == pl.num_programs(1) - 1)
    def _():
        o_all = a_scr[...] / l_scr[...]
        for h in range(n_heads):
            r0 = h * 2 * tq
            o = o_all[r0:r0 + tq] - lam_ref[0] * o_all[r0 + tq:r0 + 2 * tq]
            ms = jnp.mean(o * o, axis=-1, keepdims=True)
            o_ref[0, :, h * vd:(h + 1) * vd] = o * lax.rsqrt(ms + EPS) * g_ref[...] * out_scale


def _decode_attn(q, k_new, v_new, cache_k, cache_v, layer, page_table, table, lam, subln_g,
                 *, db, tq, lam_init):
    _, _, page, n_heads, vd = cache_k.shape
    hd = vd // 2
    d = n_heads * vd
    np_seq = page_table.shape[1]
    cpp = DECODE_PAGES_PER_STEP
    assert np_seq % cpp == 0 and cpp * page + 1 >= MAX_DISTANCE
    n_chunks = np_seq // cpp
    past = np_seq * page
    rows = n_heads * 2 * tq
    new_pad = 128 // n_heads

    q5 = q.reshape(db, tq, n_heads, 2, hd).transpose(0, 2, 3, 1, 4)
    z = jnp.zeros_like(q5[:, :, 0])
    q_bd = jnp.concatenate([jnp.concatenate([q5[:, :, 0], z], -1),
                            jnp.concatenate([z, q5[:, :, 1]], -1)], axis=2).reshape(db, rows, vd)

    row_h = jnp.repeat(jnp.arange(n_heads, dtype=I32), 2 * tq)
    row_t = jnp.tile(jnp.arange(tq, dtype=I32), 2 * n_heads)

    def bias_for(key_pos, key_valid):
        per_head = _t5_bias(table, past + row_t[:, None] - key_pos[None, :])
        per_head = jnp.where(key_valid[None, None, :], per_head, MASKED)
        own = row_h[None, :, None] == jnp.arange(n_heads, dtype=I32)[:, None, None]
        return jnp.where(own, per_head, MASKED).transpose(1, 2, 0).reshape(rows, -1)

    span = cpp * page
    far_pos = jnp.arange(span, dtype=I32)
    near_pos = past - span + far_pos
    bias_past = jnp.stack([bias_for(far_pos, far_pos >= 0), bias_for(near_pos, far_pos >= 0)])
    new_t = jnp.arange(new_pad, dtype=I32)
    bias_new = bias_for(past + new_t, new_t < tq)
    pad = ((0, 0), (0, new_pad - tq), (0, 0))
    kn = jnp.pad(k_new.reshape(db, tq, d), pad).reshape(db, new_pad * n_heads, vd)
    vn = jnp.pad(v_new.reshape(db, tq, d), pad).reshape(db, new_pad * n_heads, vd)

    def page_map(i):
        return lambda b, c, pt: (layer, pt[b * np_seq + c * cpp + i], 0, 0, 0)

    page_specs = [pl.BlockSpec((1, 1, page, n_heads, vd), page_map(i)) for i in range(cpp)]
    kern = functools.partial(_decode_kernel, n_pages=cpp, n_heads=n_heads, tq=tq,
                             out_scale=1.0 - lam_init)
    grid_spec = pltpu.PrefetchScalarGridSpec(
        num_scalar_prefetch=1,
        grid=(db, n_chunks),
        in_specs=[pl.BlockSpec(memory_space=pltpu.SMEM),
                  pl.BlockSpec((1, rows, vd), lambda b, c, pt: (b, 0, 0))]
                 + page_specs + page_specs
                 + [pl.BlockSpec((1, rows, span * n_heads),
                                 lambda b, c, pt: (jnp.where(c == n_chunks - 1, 1, 0), 0, 0)),
                    pl.BlockSpec((1, new_pad * n_heads, vd), lambda b, c, pt: (b, 0, 0)),
                    pl.BlockSpec((1, new_pad * n_heads, vd), lambda b, c, pt: (b, 0, 0)),
                    pl.BlockSpec((rows, new_pad * n_heads), lambda b, c, pt: (0, 0)),
                    pl.BlockSpec((1, vd), lambda b, c, pt: (0, 0))],
        out_specs=pl.BlockSpec((1, tq, d), lambda b, c, pt: (b, 0, 0)),
        scratch_shapes=[pltpu.VMEM((rows, 1), F32), pltpu.VMEM((rows, 1), F32),
                        pltpu.VMEM((rows, vd), F32)],
    )
    out = pl.pallas_call(
        kern,
        grid_spec=grid_spec,
        out_shape=jax.ShapeDtypeStruct((db, tq, d), F32),
        compiler_params=_params(("arbitrary", "arbitrary"), DECODE_VMEM_LIMIT),
        name="decode_attn",
    )(page_table.reshape(-1), lam, q_bd, *([cache_k] * cpp), *([cache_v] * cpp),
      bias_past, kn, vn, bias_new, subln_g.reshape(1, vd))
    return out.reshape(db * tq, d).astype(BF16)


def _merge_kernel(attn_ref, sga_ref, mconv_ref, x_ref, wo_ref, g_ref, rwh_ref, rwl_ref, rb_ref,
                  cnt_in_ref, x1_ref, h2_ref, idx_ref, gate_ref, rank_ref, cnt_out_ref, carry_scr):
    i = pl.program_id(0)

    @pl.when(i == 0)
    def _():
        carry_scr[...] = cnt_in_ref[...]

    m = sga_ref[...].astype(F32) * attn_ref[...].astype(F32) + mconv_ref[...].astype(F32)
    x1 = x_ref[...] + _dot(m.astype(BF16), wo_ref[...])
    x1_ref[...] = x1
    ms = jnp.mean(x1 * x1, axis=-1, keepdims=True)
    h2 = x1 * lax.rsqrt(ms + EPS) * g_ref[...]
    h2_ref[...] = h2

    hh = h2.astype(BF16)
    hl = (h2 - hh.astype(F32)).astype(BF16)
    logits = (_dot_nt(rwh_ref[...], hh) + _dot_nt(rwh_ref[...], hl) + _dot_nt(rwl_ref[...], hh)
              + rb_ref[...])
    n_exp, tm = logits.shape
    e_iota = lax.broadcasted_iota(I32, (n_exp, tm), 0).astype(F32)
    work = logits
    tops, sels = [], []
    for k in range(TOP_K):
        mx = jnp.max(work, axis=0, keepdims=True)
        ik = jnp.min(jnp.where(work == mx, e_iota, float(n_exp)), axis=0, keepdims=True)
        sel = e_iota == ik
        work = jnp.where(sel, -jnp.inf, work)
        tops.append(mx)
        sels.append(sel)
        idx_ref[k:k + 1, :] = ik.astype(I32)
    exps = [jnp.exp(t - tops[0]) for t in tops]
    den = exps[0] + exps[1] + exps[2] + exps[3]
    for k in range(TOP_K):
        gate_ref[k:k + 1, :] = exps[k] / den

    assigned = sels[0] | sels[1] | sels[2] | sels[3]
    a_mat = jnp.where(assigned, 1.0, 0.0).astype(BF16)
    rr = lax.broadcasted_iota(I32, (tm, tm), 0)
    cc = lax.broadcasted_iota(I32, (tm, tm), 1)
    upper = jnp.where(rr < cc, 1.0, 0.0).astype(BF16)
    before = _dot(a_mat, upper) + carry_scr[:, 0:1]
    for k in range(TOP_K):
        rank_ref[k:k + 1, :] = jnp.sum(jnp.where(sels[k], before, 0.0), axis=0,
                                       keepdims=True).astype(I32)
    carry_scr[...] = carry_scr[...] + jnp.sum(a_mat.astype(F32), axis=1, keepdims=True)
    cnt_out_ref[...] = carry_scr[...]


def _merge_route(attn, sga, mconv, x2d, wo_bf, ffn_g, rw_hi, rw_lo, rb, cnt_in):
    n, d = x2d.shape
    tm = min(MERGE_TM, n)
    assert n % tm == 0
    n_exp = rw_hi.shape[0]
    row = pl.BlockSpec((tm, d), lambda i: (i, 0))
    full = lambda shape: pl.BlockSpec(shape, lambda i: (0,) * len(shape))
    tok = pl.BlockSpec((TOP_K, tm), lambda i: (0, i))
    return pl.pallas_call(
        _merge_kernel,
        grid=(n // tm,),
        in_specs=[row, row, row, row, full((d, d)), full((1, d)), full((n_exp, d)),
                  full((n_exp, d)), full((n_exp, 1)), full((n_exp, 128))],
        out_specs=[row, row, tok, tok, tok, full((n_exp, 128))],
        out_shape=[jax.ShapeDtypeStruct((n, d), F32), jax.ShapeDtypeStruct((n, d), F32),
                   jax.ShapeDtypeStruct((TOP_K, n), I32), jax.ShapeDtypeStruct((TOP_K, n), F32),
                   jax.ShapeDtypeStruct((TOP_K, n), I32), jax.ShapeDtypeStruct((n_exp, 128), F32)],
        scratch_shapes=[pltpu.VMEM((n_exp, 128), F32)],
        compiler_params=_params(("arbitrary",)),
        name="merge_route",
    )(attn, sga, mconv, x2d, wo_bf, ffn_g.reshape(1, d), rw_hi, rw_lo, rb, cnt_in)


def _tile_indices(dest, tt):
    k, n = dest.shape
    return dest.reshape(k, n // tt, tt).transpose(1, 0, 2).reshape(n // tt, k * tt)


def _dispatch_kernel(dest_ref, h_ref, xs_in_ref, xs_ref, idx_smem, idx_sem, row_sem, *, tt):
    del xs_in_ref
    i = pl.program_id(0)
    cp = pltpu.make_async_copy(dest_ref.at[i], idx_smem, idx_sem)
    cp.start()
    cp.wait()

    def row_copy(r, k):
        return pltpu.make_async_copy(h_ref.at[pl.ds(r, 1)],
                                     xs_ref.at[pl.ds(idx_smem[k * tt + r], 1)], row_sem)

    def issue(r, carry):
        for k in range(TOP_K):
            row_copy(r, k).start()
        return carry

    def drain(r, carry):
        for k in range(TOP_K):
            row_copy(r, k).wait()
        return carry

    lax.fori_loop(0, tt, issue, 0)
    lax.fori_loop(0, tt, drain, 0)


def _dispatch(h2, dest, xs):
    n, d = h2.shape
    tt = min(DISPATCH_TT, n)
    assert n % tt == 0
    tiles = _tile_indices(dest, tt)
    return pl.pallas_call(
        functools.partial(_dispatch_kernel, tt=tt),
        grid=(n // tt,),
        in_specs=[pl.BlockSpec(tiles.shape, lambda i: (0, 0)),
                  pl.BlockSpec((tt, d), lambda i: (i, 0)), pl.BlockSpec(memory_space=pl.ANY)],
        out_specs=pl.BlockSpec(memory_space=pl.ANY),
        out_shape=jax.ShapeDtypeStruct(xs.shape, xs.dtype),
        scratch_shapes=[pltpu.SMEM((TOP_K * tt,), I32), pltpu.SemaphoreType.DMA(()),
                        pltpu.SemaphoreType.DMA(())],
        input_output_aliases={2: 0},
        compiler_params=_params(("arbitrary",)),
        name="dispatch",
    )(tiles, h2, xs)


def _moe_kernel(be_ref, nu_ref, xs_ref, wg_ref, wu_ref, wd_ref, bg_ref, bu_ref, bd_ref, y_ref,
                xb_scr):
    del be_ref
    i = pl.program_id(0)
    f = pl.program_id(1)

    @pl.when(i < nu_ref[0])
    def _():
        @pl.when(f == 0)
        def _():
            xb_scr[...] = xs_ref[...].astype(BF16)

        xb = xb_scr[...]
        gate = jnp.minimum(_dot(xb, wg_ref[0]) + bg_ref[0], SWIGLU_LIMIT)
        up = jnp.clip(_dot(xb, wu_ref[0]) + bu_ref[0], -SWIGLU_LIMIT, SWIGLU_LIMIT)
        act = (up + 1.0) * gate * jax.nn.sigmoid(SWIGLU_ALPHA * gate)
        part = _dot(act.astype(BF16), wd_ref[0])

        @pl.when(f == 0)
        def _():
            y_ref[...] = part + bd_ref[0]

        @pl.when(f > 0)
        def _():
            y_ref[...] += part

    @pl.when((i >= nu_ref[0]) & (f == 0))
    def _():
        y_ref[...] = jnp.zeros(y_ref.shape, y_ref.dtype)


def _moe_ffn(xs, block_e, n_used, wg, wu, wd, bg, bu, bd):
    rows, d = xs.shape
    n_exp, _, ff = wg.shape
    tm, fc = MOE_TM, MOE_FC
    assert rows % tm == 0 and ff % fc == 0
    nf = ff // fc

    def blk(i, nu):
        return jnp.minimum(i, nu[0] - 1)

    def fch(i, f, nu):
        return jnp.where(i < nu[0], f, nf - 1)

    grid_spec = pltpu.PrefetchScalarGridSpec(
        num_scalar_prefetch=2,
        grid=(rows // tm, nf),
        in_specs=[pl.BlockSpec((tm, d), lambda i, f, be, nu: (blk(i, nu), 0)),
                  pl.BlockSpec((1, d, fc), lambda i, f, be, nu: (be[blk(i, nu)], 0, fch(i, f, nu))),
                  pl.BlockSpec((1, d, fc), lambda i, f, be, nu: (be[blk(i, nu)], 0, fch(i, f, nu))),
                  pl.BlockSpec((1, fc, d), lambda i, f, be, nu: (be[blk(i, nu)], fch(i, f, nu), 0)),
                  pl.BlockSpec((1, 1, fc), lambda i, f, be, nu: (be[blk(i, nu)], 0, fch(i, f, nu))),
                  pl.BlockSpec((1, 1, fc), lambda i, f, be, nu: (be[blk(i, nu)], 0, fch(i, f, nu))),
                  pl.BlockSpec((1, 1, d), lambda i, f, be, nu: (be[blk(i, nu)], 0, 0))],
        out_specs=pl.BlockSpec((tm, d), lambda i, f, be, nu: (i, 0)),
        scratch_shapes=[pltpu.VMEM((tm, d), BF16)],
    )
    return pl.pallas_call(
        _moe_kernel,
        grid_spec=grid_spec,
        out_shape=jax.ShapeDtypeStruct((rows, d), F32),
        compiler_params=_params(("arbitrary", "arbitrary"), MOE_VMEM_LIMIT),
        name="moe_ffn",
    )(block_e, n_used, xs, wg, wu, wd, bg, bu, bd)


def _combine_kernel(dest_ref, x1_ref, gate_ref, g_ref, yb_ref, y_ref, buf0, buf1, idx0, idx1,
                    idx_sem, row_sem, *, tc):
    i = pl.program_id(0)

    bufs, idxs = (buf0, buf1), (idx0, idx1)

    def row_copy(s, r, k):
        return pltpu.make_async_copy(yb_ref.at[pl.ds(idxs[s][k * tc + r], 1)],
                                     bufs[s].at[k, pl.ds(r, 1)], row_sem.at[s])

    def request(tile, s):
        cp = pltpu.make_async_copy(dest_ref.at[tile], idxs[s], idx_sem)
        cp.start()
        cp.wait()

        def issue(r, carry):
            for k in range(TOP_K):
                row_copy(s, r, k).start()
            return carry

        lax.fori_loop(0, tc, issue, 0)

    def consume(s):
        def drain(r, carry):
            for k in range(TOP_K):
                row_copy(s, r, k).wait()
            return carry

        lax.fori_loop(0, tc, drain, 0)
        gates = gate_ref[...]
        y = x1_ref[...]
        for k in range(TOP_K):
            y = y + gates[:, k:k + 1] * bufs[s][k]
        ms = jnp.mean(y * y, axis=-1, keepdims=True)
        y_ref[...] = y * lax.rsqrt(ms + EPS) * g_ref[...]

    @pl.when(i == 0)
    def _():
        request(0, 0)

    for s in range(2):
        @pl.when(i % 2 == s)
        def _():
            @pl.when(i + 1 < pl.num_programs(0))
            def _():
                request(i + 1, 1 - s)

            consume(s)


def _combine(x1, gates, dest, yb, final_g):
    n, d = x1.shape
    tc = min(COMBINE_TC, n)
    assert n % tc == 0
    tiles = _tile_indices(dest, tc)
    return pl.pallas_call(
        functools.partial(_combine_kernel, tc=tc),
        grid=(n // tc,),
        in_specs=[pl.BlockSpec(tiles.shape, lambda i: (0, 0)),
                  pl.BlockSpec((tc, d), lambda i: (i, 0)),
                  pl.BlockSpec((tc, TOP_K), lambda i: (i, 0)),
                  pl.BlockSpec((1, d), lambda i: (0, 0)),
                  pl.BlockSpec(memory_space=pl.ANY)],
        out_specs=pl.BlockSpec((tc, d), lambda i: (i, 0)),
        out_shape=jax.ShapeDtypeStruct((n, d), F32),
        scratch_shapes=[pltpu.VMEM((TOP_K, tc, d), F32), pltpu.VMEM((TOP_K, tc, d), F32),
                        pltpu.SMEM((TOP_K * tc,), I32), pltpu.SMEM((TOP_K * tc,), I32),
                        pltpu.SemaphoreType.DMA(()), pltpu.SemaphoreType.DMA((2,))],
        compiler_params=_params(("arbitrary",)),
        name="combine",
    )(tiles, x1, gates.T, final_g.reshape(1, d), yb)


def _expert_offsets(idx, rank, pad_start):
    experts = jnp.arange(pad_start.shape[0], dtype=I32).reshape((-1,) + (1,) * idx.ndim)
    starts = pad_start.reshape(experts.shape)
    return rank + jnp.sum(jnp.where(idx[None] == experts, starts, 0), axis=0)


def kernel(x_prompt, x_sample, cache_k, cache_v, state_conv, page_table, attn_norm_g, w_in,
           lambda_q1, lambda_k1, lambda_q2, lambda_k2, subln_g, rel_bias_table, conv_w, w_out,
           ffn_norm_g, router_w, router_b, w_gate_up, b_gate_up, w_down, b_down, final_norm_g):
    bp, tp, d = x_prompt.shape
    bs, ts, _ = x_sample.shape
    depth, _, _, n_heads, vd = cache_v.shape
    hd = vd // 2
    n_exp = router_w.shape[-1]
    ff = w_down.shape[2]
    np_, ns_ = bp * tp, bs * ts
    assert depth == 1, "only a one-layer stack is supported"
    l = 0
    lam_init = 0.8 - 0.6 * math.exp(-0.3 * l)
    lam = (jnp.exp(jnp.sum(lambda_q1[l] * lambda_k1[l]).astype(F32))
           - jnp.exp(jnp.sum(lambda_q2[l] * lambda_k2[l]).astype(F32)) + lam_init).reshape(1)
    w_bf = _cast_bf16(w_in[l])
    wo_bf = _cast_bf16(w_out[l])
    xp = x_prompt.reshape(np_, d)
    xs_ = x_sample.reshape(ns_, d)

    qp, kp, vp, kpb, vpt, mconv_p, sga_p, st_p = _rms_proj(
        xp, attn_norm_g[l], w_bf, conv_w[l], jnp.zeros((bp, 2, d), F32), t_len=tp, head_dim=hd)
    attn_p, wg, wu, wd = _prompt_attn(
        qp, kpb, vpt, rel_bias_table, lam, subln_g[l],
        w_gate_up[l].reshape(n_exp * d, 2 * ff), w_down[l].reshape(n_exp * ff, d),
        b=bp, t=tp, n_heads=n_heads, hd=hd, lam_init=lam_init)
    wg, wu, wd = wg.reshape(n_exp, d, ff), wu.reshape(n_exp, d, ff), wd.reshape(n_exp, ff, d)
    st_prev = state_conv[l]
    prev_rows = (jnp.repeat(st_prev[:, 0], ts, axis=0), jnp.repeat(st_prev[:, 1], ts, axis=0))
    qs, ks, vs, _, _, mconv_s, sga_s, u_s = _rms_proj(
        xs_, attn_norm_g[l], w_bf, conv_w[l], prev_rows, t_len=ts, head_dim=hd)
    attn_s = _decode_attn(qs, ks, vs, cache_k, cache_v, l, page_table, rel_bias_table, lam,
                          subln_g[l], db=bs, tq=ts, lam_init=lam_init)

    rw_t = router_w[l].T
    rw_hi = rw_t.astype(BF16)
    rw_lo = (rw_t - rw_hi.astype(F32)).astype(BF16)
    rb = router_b[l].astype(F32).reshape(n_exp, 1)
    cnt0 = jnp.zeros((n_exp, 128), F32)
    x1p, h2p, idx_p, gate_p, rank_p, cnt1 = _merge_route(
        attn_p, sga_p, mconv_p, xp, wo_bf, ffn_norm_g[l], rw_hi, rw_lo, rb, cnt0)
    x1s, h2s, idx_s, gate_s, rank_s, cnt2 = _merge_route(
        attn_s, sga_s, mconv_s, xs_, wo_bf, ffn_norm_g[l], rw_hi, rw_lo, rb, cnt1)

    counts = cnt2[:, 0].astype(I32)
    padded = (counts + MOE_TM - 1) // MOE_TM * MOE_TM
    pad_end = jnp.cumsum(padded)
    pad_start = pad_end - padded
    n_blocks = -(-((np_ + ns_) * TOP_K) // MOE_TM) + n_exp
    block_start = jnp.arange(n_blocks, dtype=I32) * MOE_TM
    block_e = jnp.minimum(jnp.sum(block_start[:, None] >= pad_end[None, :], axis=1),
                          n_exp - 1).astype(I32)
    n_used = (pad_end[-1] // MOE_TM).astype(I32).reshape(1)
    dest_p = _expert_offsets(idx_p, rank_p, pad_start)
    dest_s = _expert_offsets(idx_s, rank_s, pad_start)

    xsorted = jnp.zeros((n_blocks * MOE_TM, d), F32)
    xsorted = _dispatch(h2p, dest_p, xsorted)
    xsorted = _dispatch(h2s, dest_s, xsorted)

    b_gu = b_gate_up[l].reshape(n_exp, ff, 2)
    yb = _moe_ffn(xsorted, block_e, n_used, wg, wu, wd,
                  b_gu[:, :, 0].reshape(n_exp, 1, ff), b_gu[:, :, 1].reshape(n_exp, 1, ff),
                  b_down[l].reshape(n_exp, 1, d))

    y_p = _combine(x1p, gate_p, dest_p, yb, final_norm_g)
    y_s = _combine(x1s, gate_s, dest_s, yb, final_norm_g)

    def stack(a, b, t):
        return a.reshape(depth, b, t, n_heads, vd)

    return (y_p.reshape(bp, tp, d), y_s.reshape(bs, ts, d),
            stack(kp, bp, tp), stack(vp, bp, tp), stack(ks, bs, ts), stack(vs, bs, ts),
            st_p.reshape(bp, -1, 2, d)[None, :, -1], u_s.reshape(bs, ts, d)[None, :, ts - 2:])
```

```python
import functools
import math

import jax
import jax.numpy as jnp
from jax import lax
from jax.experimental import pallas as pl
from jax.experimental.pallas import tpu as pltpu

F32 = jnp.float32
BF16 = jnp.bfloat16
I32 = jnp.int32

EPS = 1e-6
MAX_DISTANCE = 128
TOP_K = 4
SWIGLU_LIMIT = 7.0
SWIGLU_ALPHA = 1.702
MASKED = -1e30
LOG2E = math.log2(math.e)
N_SECTIONS = 8

VMEM_LIMIT = 48 * 1024 * 1024
MXU_HALF = 128
PROJ_TM, PROJ_TN = 512, 256
FAR_UNROLL = 2
DECODE_PAGES_PER_STEP = 4
DECODE_PAGE_BUFFERS = 3
DECODE_VMEM_LIMIT = 56 * 1024 * 1024
MERGE_TM = 512
MOE_TM, MOE_FC = 512, 1024
MOE_VMEM_LIMIT = 56 * 1024 * 1024
DISPATCH_TT = 512
COMBINE_TC = 128
CAST_BLOCK_BYTES = 4 * 1024 * 1024
ATTN_VMEM_LIMIT = 56 * 1024 * 1024


def _params(sem, vmem_limit=VMEM_LIMIT):
    return pltpu.CompilerParams(dimension_semantics=sem, vmem_limit_bytes=vmem_limit)


def _dot(a, b):
    return jnp.dot(a, b, preferred_element_type=F32)


def _dot_nt(a, b):
    return lax.dot_general(a, b, (((1,), (1,)), ((), ())), preferred_element_type=F32)


def _cast_kernel(x_ref, o_ref):
    o_ref[...] = x_ref[...].astype(o_ref.dtype)


def _cast_bf16(x2d):
    r, c = x2d.shape
    tr = min(CAST_BLOCK_BYTES // (4 * c), r)
    assert r % tr == 0 and tr % 16 == 0
    return pl.pallas_call(
        _cast_kernel,
        grid=(r // tr,),
        in_specs=[pl.BlockSpec((tr, c), lambda i: (i, 0))],
        out_specs=pl.BlockSpec((tr, c), lambda i: (i, 0)),
        out_shape=jax.ShapeDtypeStruct((r, c), BF16),
        compiler_params=_params(("arbitrary",)),
        name="cast_bf16",
    )(x2d)


def _proj_kernel(*refs, carry_mode, t_tiles, t_len, q_scale):
    x_ref, g_ref = refs[0:2]
    w_refs = refs[2:2 + N_SECTIONS]
    cw_ref = refs[2 + N_SECTIONS]
    if carry_mode:
        prev_ref = refs[3 + N_SECTIONS]
        outs = refs[4 + N_SECTIONS:]
    else:
        p0_ref, p1_ref = refs[3 + N_SECTIONS:5 + N_SECTIONS]
        outs = refs[5 + N_SECTIONS:]
    q_out, k_out, v_out, kb_out, vt_out, mconv_out, sga_out, st_out, h_scr, carry_scr = outs
    i = pl.program_id(0)
    j = pl.program_id(1)

    @pl.when(j == 0)
    def _():
        x = x_ref[...]
        ms = jnp.mean(x * x, axis=-1, keepdims=True)
        h_scr[...] = (x * lax.rsqrt(ms + EPS) * g_ref[...]).astype(BF16)

    h = h_scr[...]
    wq, wk, wv, wcb, wcc, wch, wga, wgb = w_refs
    q_out[...] = (_dot(h, wq[...]) * q_scale).astype(BF16)
    k = _dot(h, wk[...])
    k_out[...] = k
    kb_out[...] = k.astype(BF16)
    v = _dot(h, wv[...])
    v_out[...] = v
    vt_out[0, 0] = v.T.astype(BF16)
    sga_out[...] = jax.nn.sigmoid(_dot(h, wga[...])).astype(BF16)

    u = _dot(h, wcc[...]) * _dot(h, wch[...])
    tm = u.shape[0]
    r = lax.broadcasted_iota(I32, u.shape, 0)
    d1 = pltpu.roll(u, 1, 0)
    d2 = pltpu.roll(u, 2, 0)
    if carry_mode:
        @pl.when(i % t_tiles == 0)
        def _():
            carry_scr[j] = prev_ref[0]
        c = carry_scr[j]
        c0, c1 = c[0:1, :], c[1:2, :]
        s1 = jnp.where(r == 0, c1, d1)
        s2 = jnp.where(r == 0, c0, jnp.where(r == 1, c1, d2))
        carry_scr[j] = u[tm - 2:tm, :]
        st_out[0] = u[tm - 2:tm, :]
    else:
        t = r % t_len
        p0, p1 = p0_ref[...], p1_ref[...]
        s1 = jnp.where(t == 0, p1, d1)
        s2 = jnp.where(t == 0, p0, jnp.where(t == 1, p1, d2))
        st_out[...] = u
    cw = cw_ref[...]
    conv = s2 * cw[0:1, :] + s1 * cw[1:2, :] + u * cw[2:3, :]
    gate_b = _dot(h, wcb[...])
    mconv_out[...] = (jax.nn.sigmoid(_dot(h, wgb[...])) * gate_b * conv).astype(BF16)


def _rms_proj(x2d, norm_g, w_bf, conv_w, prev, *, t_len, head_dim):
    n, d = x2d.shape
    tm, tn = min(PROJ_TM, n), PROJ_TN
    assert n % tm == 0 and d % tn == 0 and t_len >= 2
    carry_mode = t_len % tm == 0
    n_col = d // tn
    row_blk = pl.BlockSpec((tm, tn), lambda i, j: (i, j))
    in_specs = [pl.BlockSpec((tm, d), lambda i, j: (i, 0)),
                pl.BlockSpec((1, d), lambda i, j: (0, 0))]
    in_specs += [pl.BlockSpec((d, tn), functools.partial(lambda i, j, s: (0, s * n_col + j), s=s))
                 for s in range(N_SECTIONS)]
    in_specs.append(pl.BlockSpec((3, tn), lambda i, j: (0, j)))
    if carry_mode:
        t_tiles = t_len // tm
        in_specs.append(pl.BlockSpec((1, 2, tn), lambda i, j: (i // t_tiles, 0, j)))
        prev_args = (prev,)
        st_shape = jax.ShapeDtypeStruct((n // tm, 2, d), F32)
        st_spec = pl.BlockSpec((1, 2, tn), lambda i, j: (i, 0, j))
    else:
        assert tm % t_len == 0
        t_tiles = 1
        in_specs += [row_blk, row_blk]
        prev_args = tuple(prev)
        st_shape = jax.ShapeDtypeStruct((n, d), F32)
        st_spec = row_blk
    row_bf = jax.ShapeDtypeStruct((n, d), BF16)
    row_f32 = jax.ShapeDtypeStruct((n, d), F32)
    out_shape = [row_bf, row_f32, row_f32, row_bf,
                 jax.ShapeDtypeStruct((n // tm, n_col, tn, tm), BF16), row_bf, row_bf, st_shape]
    vt_spec = pl.BlockSpec((1, 1, tn, tm), lambda i, j: (i, j, 0, 0))
    kern = functools.partial(_proj_kernel, carry_mode=carry_mode, t_tiles=t_tiles, t_len=t_len,
                             q_scale=head_dim ** -0.5 * LOG2E)
    return pl.pallas_call(
        kern,
        grid=(n // tm, n_col),
        in_specs=in_specs,
        out_specs=[row_blk] * 4 + [vt_spec, row_blk, row_blk, st_spec],
        out_shape=out_shape,
        scratch_shapes=[pltpu.VMEM((tm, d), BF16), pltpu.VMEM((n_col, 2, tn), F32)],
        compiler_params=_params(("arbitrary", "arbitrary")),
        name="rms_proj",
    )(x2d, norm_g.reshape(1, d), *([w_bf] * N_SECTIONS), conv_w, *prev_args)


def _t5_bias(table, dist):
    nb = table.shape[0]
    n = jnp.maximum(dist, 0)
    max_exact = nb // 2
    large = max_exact + (jnp.log(jnp.maximum(n, 1).astype(F32) / max_exact)
                         / math.log(MAX_DISTANCE / max_exact) * (nb - max_exact)).astype(I32)
    bucket = jnp.where(n < max_exact, n, jnp.minimum(large, nb - 1))
    per_bucket = (table.astype(F32) * LOG2E).reshape((nb, table.shape[1]) + (1,) * dist.ndim)
    buckets = jnp.arange(nb, dtype=I32).reshape((nb, 1) + (1,) * dist.ndim)
    bias = jnp.sum(jnp.where(bucket[None, None] == buckets, per_bucket, 0.0), axis=0)
    return jnp.where(dist >= 0, bias, MASKED)


def _split_gate_up_block(w_ref, g_ref, u_ref):
    lanes = MXU_HALF
    width = 2 * lanes
    w = w_ref[...].astype(BF16)
    rr = lax.broadcasted_iota(I32, (width, width), 0)
    cc = lax.broadcasted_iota(I32, (width, width), 1)
    src = jnp.where(cc < lanes, 2 * cc, 2 * (cc - lanes) + 1)
    perm = jnp.where(rr == src, 1.0, 0.0).astype(BF16)
    for b in range(w.shape[1] // width):
        moved = _dot(w[:, b * width:(b + 1) * width], perm).astype(BF16)
        g_ref[:, b * lanes:(b + 1) * lanes] = moved[:, :lanes]
        u_ref[:, b * lanes:(b + 1) * lanes] = moved[:, lanes:]


def _pattn_kernel(lam_ref, cfar_ref, q_ref, k_ref, vt_ref, bd_ref, bs_ref, g_ref, wgu_ref, wdn_ref,
                  o_ref, wg_out, wu_out, wd_out, a_scr, *, tb, hd, out_scale):
    h = pl.program_id(1)
    qi = pl.program_id(2)
    _split_gate_up_block(wgu_ref, wg_out, wu_out)
    wd_out[...] = wdn_ref[...].astype(BF16)

    q = q_ref[...]
    qs = (q[:, :hd], q[:, hd:])
    cfar = cfar_ref[h]
    a_scr[...] = jnp.zeros(a_scr.shape, F32)

    def block(j, stats, bias):
        start = pl.multiple_of(j * tb, tb)
        kk = k_ref[pl.ds(start, tb), :]
        vt = vt_ref[j, 0]
        sts = [_dot_nt(kk[:, mi * hd:(mi + 1) * hd], qs[mi]) for mi in range(2)]
        new_stats, ps, alphas = [], [], []
        for mi in range(2):
            m_old, l_old = stats[mi]
            st = sts[mi]
            if bias is None:
                m_new = jnp.maximum(m_old, jnp.max(st, axis=0, keepdims=True) + cfar)
                p = jnp.exp2(st - (m_new - cfar))
            else:
                st = st + bias
                m_new = jnp.maximum(m_old, jnp.max(st, axis=0, keepdims=True))
                p = jnp.exp2(st - m_new)
            alpha = jnp.exp2(m_old - m_new)
            new_stats.append((m_new, alpha * l_old + jnp.sum(p, axis=0, keepdims=True)))
            ps.append(p.astype(BF16))
            alphas.append(alpha)
        for mi in range(2):
            a_scr[mi] = alphas[mi] * a_scr[mi] + _dot(vt, ps[mi])
        return tuple(new_stats)

    init = (jnp.full((1, tb), MASKED, F32), jnp.zeros((1, tb), F32))
    n_far = jnp.maximum(qi - 1, 0)

    def far_group(j, s):
        for u in range(FAR_UNROLL):
            s = block(FAR_UNROLL * j + u, s, None)
        return s

    n_group = n_far // FAR_UNROLL
    stats = lax.fori_loop(0, n_group, far_group, (init, init))
    stats = lax.fori_loop(n_group * FAR_UNROLL, n_far, lambda j, s: block(j, s, None), stats)
    stats = lax.cond(qi >= 1, lambda s: block(qi - 1, s, bs_ref[0]), lambda s: s, stats)
    (_, l1), (_, l2) = block(qi, stats, bd_ref[0])

    ot = a_scr[0] * (1.0 / l1) - lam_ref[0] * (a_scr[1] * (1.0 / l2))
    ms = jnp.mean(ot * ot, axis=0, keepdims=True)
    ot = ot * lax.rsqrt(ms + EPS) * (g_ref[...] * out_scale)
    o_ref[...] = ot.T.astype(o_ref.dtype)


def _prompt_attn(q, kb, vt, table, lam, subln_g, w_gu, w_dn, *, b, t, n_heads, hd, lam_init):
    n, d = q.shape
    vd = 2 * hd
    tb = vt.shape[3]
    assert t % tb == 0 and tb + 1 >= MAX_DISTANCE and vt.shape == (n // tb, n_heads, vd, tb)
    nq = t // tb
    steps = b * n_heads * nq
    gu_rows, dn_rows = w_gu.shape[0] // steps, w_dn.shape[0] // steps
    assert gu_rows * steps == w_gu.shape[0] and dn_rows * steps == w_dn.shape[0]
    assert gu_rows % 16 == 0 and dn_rows % 16 == 0 and w_gu.shape[1] % (4 * MXU_HALF) == 0
    ff = w_gu.shape[1] // 2

    def slab(rows, cols):
        return pl.BlockSpec((rows, cols), lambda bi, h, i: ((bi * n_heads + h) * nq + i, 0))

    pos = jnp.arange(tb, dtype=I32)
    dist = pos[None, :] - pos[:, None]
    bias_diag = _t5_bias(table, dist)
    bias_sub = _t5_bias(table, tb + dist)
    cfar = table[table.shape[0] - 1].astype(F32) * LOG2E
    kern = functools.partial(_pattn_kernel, tb=tb, hd=hd, out_scale=1.0 - lam_init)
    smem = pl.BlockSpec(memory_space=pltpu.SMEM)
    return pl.pallas_call(
        kern,
        grid=(b, n_heads, nq),
        in_specs=[smem, smem,
                  pl.BlockSpec((tb, vd), lambda bi, h, i: (bi * nq + i, h)),
                  pl.BlockSpec((t, vd), lambda bi, h, i: (bi, h)),
                  pl.BlockSpec((nq, 1, vd, tb), lambda bi, h, i: (bi, h, 0, 0)),
                  pl.BlockSpec((1, tb, tb), lambda bi, h, i: (h, 0, 0)),
                  pl.BlockSpec((1, tb, tb), lambda bi, h, i: (h, 0, 0)),
                  pl.BlockSpec((vd, 1), lambda bi, h, i: (0, 0)),
                  slab(gu_rows, 2 * ff), slab(dn_rows, w_dn.shape[1])],
        out_specs=[pl.BlockSpec((tb, vd), lambda bi, h, i: (bi * nq + i, h)),
                   slab(gu_rows, ff), slab(gu_rows, ff), slab(dn_rows, w_dn.shape[1])],
        out_shape=[jax.ShapeDtypeStruct((n, d), BF16),
                   jax.ShapeDtypeStruct((w_gu.shape[0], ff), BF16),
                   jax.ShapeDtypeStruct((w_gu.shape[0], ff), BF16),
                   jax.ShapeDtypeStruct(w_dn.shape, BF16)],
        scratch_shapes=[pltpu.VMEM((2, vd, tb), F32)],
        compiler_params=_params(("arbitrary", "arbitrary", "arbitrary"), ATTN_VMEM_LIMIT),
        name="prompt_attn",
    )(lam, cfar, q, kb, vt, bias_diag, bias_sub, subln_g.reshape(vd, 1), w_gu, w_dn)


def _decode_kernel(pt_ref, lam_ref, q_ref, ck_ref, cv_ref, bias_ref, kn_ref, vn_ref, bn_ref, g_ref,
                   o_ref, kbuf, vbuf, page_sem, m_scr, l_scr, a_scr,
                   *, layer, n_pages, np_seq, n_heads, tq, out_scale):
    b = pl.program_id(0)
    c = pl.program_id(1)
    n_chunks = pl.num_programs(1)
    step = b * n_chunks + c
    total = pl.num_programs(0) * n_chunks
    n_slots = kbuf.shape[0]
    q = q_ref[0]
    page, _, vd = kbuf.shape[2:]
    keys = page * n_heads

    def page_copies(t, slot):
        first = (t // n_chunks) * np_seq + (t % n_chunks) * n_pages
        copies = []
        for i in range(n_pages):
            pg = pt_ref[first + i]
            copies.append(pltpu.make_async_copy(ck_ref.at[layer, pg], kbuf.at[slot, i],
                                                page_sem.at[slot]))
            copies.append(pltpu.make_async_copy(cv_ref.at[layer, pg], vbuf.at[slot, i],
                                                page_sem.at[slot]))
        return copies

    @pl.when(step == 0)
    def _():
        for t in range(n_slots - 1):
            @pl.when(t < total)
            def _():
                for cp in page_copies(t, t):
                    cp.start()

    ahead = step + n_slots - 1

    @pl.when(ahead < total)
    def _():
        for cp in page_copies(ahead, ahead % n_slots):
            cp.start()

    slot = step % n_slots
    for cp in page_copies(step, slot):
        cp.wait()

    @pl.when(c == 0)
    def _():
        s = _dot_nt(q, kn_ref[0].astype(BF16)) + bn_ref[...]
        m = jnp.max(s, axis=-1, keepdims=True)
        p = jnp.exp2(s - m)
        m_scr[...] = m
        l_scr[...] = jnp.sum(p, axis=-1, keepdims=True)
        a_scr[...] = _dot(p.astype(BF16), vn_ref[0].astype(BF16))

    s = jnp.concatenate([_dot_nt(q, kbuf[slot, i].reshape(keys, vd).astype(BF16))
                         for i in range(n_pages)], axis=1) + bias_ref[0]
    m_old = m_scr[...]
    m_new = jnp.maximum(m_old, jnp.max(s, axis=-1, keepdims=True))
    p = jnp.exp2(s - m_new)
    alpha = jnp.exp2(m_old - m_new)
    pb = p.astype(BF16)
    pv = _dot(pb[:, 0:keys], vbuf[slot, 0].reshape(keys, vd).astype(BF16))
    for i in range(1, n_pages):
        pv = pv + _dot(pb[:, i * keys:(i + 1) * keys], vbuf[slot, i].reshape(keys, vd).astype(BF16))
    l_scr[...] = alpha * l_scr[...] + jnp.sum(p, axis=-1, keepdims=True)
    a_scr[...] = alpha * a_scr[...] + pv
    m_scr[...] = m_new

    @pl.when(c == n_chunks - 1)
    def _():
        o_all = a_scr[...] / l_scr[...]
        for h in range(n_heads):
            r0 = h * 2 * tq
            o = o_all[r0:r0 + tq] - lam_ref[0] * o_all[r0 + tq:r0 + 2 * tq]
            ms = jnp.mean(o * o, axis=-1, keepdims=True)
            o_ref[0, :, h * vd:(h + 1) * vd] = o * lax.rsqrt(ms + EPS) * g_ref[...] * out_scale


def _decode_attn(q, k_new, v_new, cache_k, cache_v, layer, page_table, table, lam, subln_g,
                 *, db, tq, lam_init):
    _, _, page, n_heads, vd = cache_k.shape
    hd = vd // 2
    d = n_heads * vd
    np_seq = page_table.shape[1]
    cpp = DECODE_PAGES_PER_STEP
    assert np_seq % cpp == 0 and cpp * page + 1 >= MAX_DISTANCE
    n_chunks = np_seq // cpp
    past = np_seq * page
    rows = n_heads * 2 * tq
    new_pad = 128 // n_heads

    q5 = q.reshape(db, tq, n_heads, 2, hd).transpose(0, 2, 3, 1, 4)
    z = jnp.zeros_like(q5[:, :, 0])
    q_bd = jnp.concatenate([jnp.concatenate([q5[:, :, 0], z], -1),
                            jnp.concatenate([z, q5[:, :, 1]], -1)], axis=2).reshape(db, rows, vd)

    row_h = jnp.repeat(jnp.arange(n_heads, dtype=I32), 2 * tq)
    row_t = jnp.tile(jnp.arange(tq, dtype=I32), 2 * n_heads)

    def bias_for(key_pos, key_valid):
        per_head = _t5_bias(table, past + row_t[:, None] - key_pos[None, :])
        per_head = jnp.where(key_valid[None, None, :], per_head, MASKED)
        own = row_h[None, :, None] == jnp.arange(n_heads, dtype=I32)[:, None, None]
        return jnp.where(own, per_head, MASKED).transpose(1, 2, 0).reshape(rows, -1)

    span = cpp * page
    far_pos = jnp.arange(span, dtype=I32)
    near_pos = past - span + far_pos
    bias_past = jnp.stack([bias_for(far_pos, far_pos >= 0), bias_for(near_pos, far_pos >= 0)])
    new_t = jnp.arange(new_pad, dtype=I32)
    bias_new = bias_for(past + new_t, new_t < tq)
    pad = ((0, 0), (0, new_pad - tq), (0, 0))
    kn = jnp.pad(k_new.reshape(db, tq, d), pad).reshape(db, new_pad * n_heads, vd)
    vn = jnp.pad(v_new.reshape(db, tq, d), pad).reshape(db, new_pad * n_heads, vd)

    kern = functools.partial(_decode_kernel, layer=layer, n_pages=cpp, np_seq=np_seq,
                             n_heads=n_heads, tq=tq, out_scale=1.0 - lam_init)
    hbm = pl.BlockSpec(memory_space=pl.ANY)
    grid_spec = pltpu.PrefetchScalarGridSpec(
        num_scalar_prefetch=1,
        grid=(db, n_chunks),
        in_specs=[pl.BlockSpec(memory_space=pltpu.SMEM),
                  pl.BlockSpec((1, rows, vd), lambda b, c, pt: (b, 0, 0)),
                  hbm, hbm,
                  pl.BlockSpec((1, rows, span * n_heads),
                               lambda b, c, pt: (jnp.where(c == n_chunks - 1, 1, 0), 0, 0)),
                  pl.BlockSpec((1, new_pad * n_heads, vd), lambda b, c, pt: (b, 0, 0)),
                  pl.BlockSpec((1, new_pad * n_heads, vd), lambda b, c, pt: (b, 0, 0)),
                  pl.BlockSpec((rows, new_pad * n_heads), lambda b, c, pt: (0, 0)),
                  pl.BlockSpec((1, vd), lambda b, c, pt: (0, 0))],
        out_specs=pl.BlockSpec((1, tq, d), lambda b, c, pt: (b, 0, 0)),
        scratch_shapes=[pltpu.VMEM((DECODE_PAGE_BUFFERS, cpp, page, n_heads, vd), F32),
                        pltpu.VMEM((DECODE_PAGE_BUFFERS, cpp, page, n_heads, vd), F32),
                        pltpu.SemaphoreType.DMA((DECODE_PAGE_BUFFERS,)),
                        pltpu.VMEM((rows, 1), F32), pltpu.VMEM((rows, 1), F32),
                        pltpu.VMEM((rows, vd), F32)],
    )
    out = pl.pallas_call(
        kern,
        grid_spec=grid_spec,
        out_shape=jax.ShapeDtypeStruct((db, tq, d), F32),
        compiler_params=_params(("arbitrary", "arbitrary"), DECODE_VMEM_LIMIT),
        name="decode_attn",
    )(page_table.reshape(-1), lam, q_bd, cache_k, cache_v, bias_past, kn, vn, bias_new,
      subln_g.reshape(1, vd))
    return out.reshape(db * tq, d).astype(BF16)


def _merge_kernel(attn_ref, sga_ref, mconv_ref, x_ref, wo_ref, g_ref, rwh_ref, rwl_ref, rb_ref,
                  cnt_in_ref, x1_ref, h2_ref, idx_ref, gate_ref, rank_ref, cnt_out_ref, carry_scr):
    i = pl.program_id(0)

    @pl.when(i == 0)
    def _():
        carry_scr[...] = cnt_in_ref[...]

    m = sga_ref[...].astype(F32) * attn_ref[...].astype(F32) + mconv_ref[...].astype(F32)
    x1 = x_ref[...] + _dot(m.astype(BF16), wo_ref[...])
    x1_ref[...] = x1
    ms = jnp.mean(x1 * x1, axis=-1, keepdims=True)
    h2 = x1 * lax.rsqrt(ms + EPS) * g_ref[...]
    h2_ref[...] = h2

    hh = h2.astype(BF16)
    hl = (h2 - hh.astype(F32)).astype(BF16)
    logits = (_dot_nt(rwh_ref[...], hh) + _dot_nt(rwh_ref[...], hl) + _dot_nt(rwl_ref[...], hh)
              + rb_ref[...])
    n_exp, tm = logits.shape
    e_iota = lax.broadcasted_iota(I32, (n_exp, tm), 0).astype(F32)
    work = logits
    tops, sels = [], []
    for k in range(TOP_K):
        mx = jnp.max(work, axis=0, keepdims=True)
        ik = jnp.min(jnp.where(work == mx, e_iota, float(n_exp)), axis=0, keepdims=True)
        sel = e_iota == ik
        work = jnp.where(sel, -jnp.inf, work)
        tops.append(mx)
        sels.append(sel)
        idx_ref[k:k + 1, :] = ik.astype(I32)
    exps = [jnp.exp(t - tops[0]) for t in tops]
    den = exps[0] + exps[1] + exps[2] + exps[3]
    for k in range(TOP_K):
        gate_ref[k:k + 1, :] = exps[k] / den

    assigned = sels[0] | sels[1] | sels[2] | sels[3]
    a_mat = jnp.where(assigned, 1.0, 0.0).astype(BF16)
    rr = lax.broadcasted_iota(I32, (tm, tm), 0)
    cc = lax.broadcasted_iota(I32, (tm, tm), 1)
    upper = jnp.where(rr < cc, 1.0, 0.0).astype(BF16)
    before = _dot(a_mat, upper) + carry_scr[:, 0:1]
    for k in range(TOP_K):
        rank_ref[k:k + 1, :] = jnp.sum(jnp.where(sels[k], before, 0.0), axis=0,
                                       keepdims=True).astype(I32)
    carry_scr[...] = carry_scr[...] + jnp.sum(a_mat.astype(F32), axis=1, keepdims=True)
    cnt_out_ref[...] = carry_scr[...]


def _merge_route(attn, sga, mconv, x2d, wo_bf, ffn_g, rw_hi, rw_lo, rb, cnt_in):
    n, d = x2d.shape
    tm = min(MERGE_TM, n)
    assert n % tm == 0
    n_exp = rw_hi.shape[0]
    row = pl.BlockSpec((tm, d), lambda i: (i, 0))
    full = lambda shape: pl.BlockSpec(shape, lambda i: (0,) * len(shape))
    tok = pl.BlockSpec((TOP_K, tm), lambda i: (0, i))
    return pl.pallas_call(
        _merge_kernel,
        grid=(n // tm,),
        in_specs=[row, row, row, row, full((d, d)), full((1, d)), full((n_exp, d)),
                  full((n_exp, d)), full((n_exp, 1)), full((n_exp, 128))],
        out_specs=[row, row, tok, tok, tok, full((n_exp, 128))],
        out_shape=[jax.ShapeDtypeStruct((n, d), F32), jax.ShapeDtypeStruct((n, d), F32),
                   jax.ShapeDtypeStruct((TOP_K, n), I32), jax.ShapeDtypeStruct((TOP_K, n), F32),
                   jax.ShapeDtypeStruct((TOP_K, n), I32), jax.ShapeDtypeStruct((n_exp, 128), F32)],
        scratch_shapes=[pltpu.VMEM((n_exp, 128), F32)],
        compiler_params=_params(("arbitrary",)),
        name="merge_route",
    )(attn, sga, mconv, x2d, wo_bf, ffn_g.reshape(1, d), rw_hi, rw_lo, rb, cnt_in)


def _tile_indices(dest, tt):
    k, n = dest.shape
    return dest.reshape(k, n // tt, tt).transpose(1, 0, 2).reshape(n // tt, k * tt)


def _dispatch_kernel(dest_ref, h_ref, xs_in_ref, xs_ref, idx_smem, idx_sem, row_sem, *, tt):
    del xs_in_ref
    i = pl.program_id(0)
    cp = pltpu.make_async_copy(dest_ref.at[i], idx_smem, idx_sem)
    cp.start()
    cp.wait()

    def row_copy(r, k):
        return pltpu.make_async_copy(h_ref.at[pl.ds(r, 1)],
                                     xs_ref.at[pl.ds(idx_smem[k * tt + r], 1)], row_sem)

    def issue(r, carry):
        for k in range(TOP_K):
            row_copy(r, k).start()
        return carry

    def drain(r, carry):
        for k in range(TOP_K):
            row_copy(r, k).wait()
        return carry

    lax.fori_loop(0, tt, issue, 0)
    lax.fori_loop(0, tt, drain, 0)


def _dispatch(h2, dest, xs):
    n, d = h2.shape
    tt = min(DISPATCH_TT, n)
    assert n % tt == 0
    tiles = _tile_indices(dest, tt)
    return pl.pallas_call(
        functools.partial(_dispatch_kernel, tt=tt),
        grid=(n // tt,),
        in_specs=[pl.BlockSpec(tiles.shape, lambda i: (0, 0)),
                  pl.BlockSpec((tt, d), lambda i: (i, 0)), pl.BlockSpec(memory_space=pl.ANY)],
        out_specs=pl.BlockSpec(memory_space=pl.ANY),
        out_shape=jax.ShapeDtypeStruct(xs.shape, xs.dtype),
        scratch_shapes=[pltpu.SMEM((TOP_K * tt,), I32), pltpu.SemaphoreType.DMA(()),
                        pltpu.SemaphoreType.DMA(())],
        input_output_aliases={2: 0},
        compiler_params=_params(("arbitrary",)),
        name="dispatch",
    )(tiles, h2, xs)


def _moe_kernel(be_ref, nu_ref, xs_ref, wg_ref, wu_ref, wd_ref, bg_ref, bu_ref, bd_ref, y_ref,
                xb_scr):
    del be_ref
    i = pl.program_id(0)
    f = pl.program_id(1)

    @pl.when(i < nu_ref[0])
    def _():
        @pl.when(f == 0)
        def _():
            xb_scr[...] = xs_ref[...].astype(BF16)

        xb = xb_scr[...]
        gate = jnp.minimum(_dot(xb, wg_ref[0]) + bg_ref[0], SWIGLU_LIMIT)
        up = jnp.clip(_dot(xb, wu_ref[0]) + bu_ref[0], -SWIGLU_LIMIT, SWIGLU_LIMIT)
        act = (up + 1.0) * gate * jax.nn.sigmoid(SWIGLU_ALPHA * gate)
        part = _dot(act.astype(BF16), wd_ref[0])

        @pl.when(f == 0)
        def _():
            y_ref[...] = part + bd_ref[0]

        @pl.when(f > 0)
        def _():
            y_ref[...] += part

    @pl.when((i >= nu_ref[0]) & (f == 0))
    def _():
        y_ref[...] = jnp.zeros(y_ref.shape, y_ref.dtype)


def _moe_ffn(xs, block_e, n_used, wg, wu, wd, bg, bu, bd):
    rows, d = xs.shape
    n_exp, _, ff = wg.shape
    tm, fc = MOE_TM, MOE_FC
    assert rows % tm == 0 and ff % fc == 0
    nf = ff // fc

    def blk(i, nu):
        return jnp.minimum(i, nu[0] - 1)

    def fch(i, f, nu):
        return jnp.where(i < nu[0], f, nf - 1)

    grid_spec = pltpu.PrefetchScalarGridSpec(
        num_scalar_prefetch=2,
        grid=(rows // tm, nf),
        in_specs=[pl.BlockSpec((tm, d), lambda i, f, be, nu: (blk(i, nu), 0)),
                  pl.BlockSpec((1, d, fc), lambda i, f, be, nu: (be[blk(i, nu)], 0, fch(i, f, nu))),
                  pl.BlockSpec((1, d, fc), lambda i, f, be, nu: (be[blk(i, nu)], 0, fch(i, f, nu))),
                  pl.BlockSpec((1, fc, d), lambda i, f, be, nu: (be[blk(i, nu)], fch(i, f, nu), 0)),
                  pl.BlockSpec((1, 1, fc), lambda i, f, be, nu: (be[blk(i, nu)], 0, fch(i, f, nu))),
                  pl.BlockSpec((1, 1, fc), lambda i, f, be, nu: (be[blk(i, nu)], 0, fch(i, f, nu))),
                  pl.BlockSpec((1, 1, d), lambda i, f, be, nu: (be[blk(i, nu)], 0, 0))],
        out_specs=pl.BlockSpec((tm, d), lambda i, f, be, nu: (i, 0)),
        scratch_shapes=[pltpu.VMEM((tm, d), BF16)],
    )
    return pl.pallas_call(
        _moe_kernel,
        grid_spec=grid_spec,
        out_shape=jax.ShapeDtypeStruct((rows, d), F32),
        compiler_params=_params(("arbitrary", "arbitrary"), MOE_VMEM_LIMIT),
        name="moe_ffn",
    )(block_e, n_used, xs, wg, wu, wd, bg, bu, bd)


def _combine_kernel(dest_ref, x1_ref, gate_ref, g_ref, yb_ref, y_ref, buf0, buf1, idx0, idx1,
                    idx_sem, row_sem, *, tc):
    i = pl.program_id(0)

    bufs, idxs = (buf0, buf1), (idx0, idx1)

    def row_copy(s, r, k):
        return pltpu.make_async_copy(yb_ref.at[pl.ds(idxs[s][k * tc + r], 1)],
                                     bufs[s].at[k, pl.ds(r, 1)], row_sem.at[s])

    def request(tile, s):
        cp = pltpu.make_async_copy(dest_ref.at[tile], idxs[s], idx_sem)
        cp.start()
        cp.wait()

        def issue(r, carry):
            for k in range(TOP_K):
                row_copy(s, r, k).start()
            return carry

        lax.fori_loop(0, tc, issue, 0)

    def consume(s):
        def drain(r, carry):
            for k in range(TOP_K):
                row_copy(s, r, k).wait()
            return carry

        lax.fori_loop(0, tc, drain, 0)
        gates = gate_ref[...]
        y = x1_ref[...]
        for k in range(TOP_K):
            y = y + gates[:, k:k + 1] * bufs[s][k]
        ms = jnp.mean(y * y, axis=-1, keepdims=True)
        y_ref[...] = y * lax.rsqrt(ms + EPS) * g_ref[...]

    @pl.when(i == 0)
    def _():
        request(0, 0)

    for s in range(2):
        @pl.when(i % 2 == s)
        def _():
            @pl.when(i + 1 < pl.num_programs(0))
            def _():
                request(i + 1, 1 - s)

            consume(s)


def _combine(x1, gates, dest, yb, final_g):
    n, d = x1.shape
    tc = min(COMBINE_TC, n)
    assert n % tc == 0
    tiles = _tile_indices(dest, tc)
    return pl.pallas_call(
        functools.partial(_combine_kernel, tc=tc),
        grid=(n // tc,),
        in_specs=[pl.BlockSpec(tiles.shape, lambda i: (0, 0)),
                  pl.BlockSpec((tc, d), lambda i: (i, 0)),
                  pl.BlockSpec((tc, TOP_K), lambda i: (i, 0)),
                  pl.BlockSpec((1, d), lambda i: (0, 0)),
                  pl.BlockSpec(memory_space=pl.ANY)],
        out_specs=pl.BlockSpec((tc, d), lambda i: (i, 0)),
        out_shape=jax.ShapeDtypeStruct((n, d), F32),
        scratch_shapes=[pltpu.VMEM((TOP_K, tc, d), F32), pltpu.VMEM((TOP_K, tc, d), F32),
                        pltpu.SMEM((TOP_K * tc,), I32), pltpu.SMEM((TOP_K * tc,), I32),
                        pltpu.SemaphoreType.DMA(()), pltpu.SemaphoreType.DMA((2,))],
        compiler_params=_params(("arbitrary",)),
        name="combine",
    )(tiles, x1, gates.T, final_g.reshape(1, d), yb)


def _expert_offsets(idx, rank, pad_start):
    experts = jnp.arange(pad_start.shape[0], dtype=I32).reshape((-1,) + (1,) * idx.ndim)
    starts = pad_start.reshape(experts.shape)
    return rank + jnp.sum(jnp.where(idx[None] == experts, starts, 0), axis=0)


def kernel(x_prompt, x_sample, cache_k, cache_v, state_conv, page_table, attn_norm_g, w_in,
           lambda_q1, lambda_k1, lambda_q2, lambda_k2, subln_g, rel_bias_table, conv_w, w_out,
           ffn_norm_g, router_w, router_b, w_gate_up, b_gate_up, w_down, b_down, final_norm_g):
    bp, tp, d = x_prompt.shape
    bs, ts, _ = x_sample.shape
    depth, _, _, n_heads, vd = cache_v.shape
    hd = vd // 2
    n_exp = router_w.shape[-1]
    ff = w_down.shape[2]
    np_, ns_ = bp * tp, bs * ts
    assert depth == 1, "only a one-layer stack is supported"
    l = 0
    lam_init = 0.8 - 0.6 * math.exp(-0.3 * l)
    lam = (jnp.exp(jnp.sum(lambda_q1[l] * lambda_k1[l]).astype(F32))
           - jnp.exp(jnp.sum(lambda_q2[l] * lambda_k2[l]).astype(F32)) + lam_init).reshape(1)
    w_bf = _cast_bf16(w_in[l])
    wo_bf = _cast_bf16(w_out[l])
    xp = x_prompt.reshape(np_, d)
    xs_ = x_sample.reshape(ns_, d)

    qp, kp, vp, kpb, vpt, mconv_p, sga_p, st_p = _rms_proj(
        xp, attn_norm_g[l], w_bf, conv_w[l], jnp.zeros((bp, 2, d), F32), t_len=tp, head_dim=hd)
    attn_p, wg, wu, wd = _prompt_attn(
        qp, kpb, vpt, rel_bias_table, lam, subln_g[l],
        w_gate_up[l].reshape(n_exp * d, 2 * ff), w_down[l].reshape(n_exp * ff, d),
        b=bp, t=tp, n_heads=n_heads, hd=hd, lam_init=lam_init)
    wg, wu, wd = wg.reshape(n_exp, d, ff), wu.reshape(n_exp, d, ff), wd.reshape(n_exp, ff, d)
    st_prev = state_conv[l]
    prev_rows = (jnp.repeat(st_prev[:, 0], ts, axis=0), jnp.repeat(st_prev[:, 1], ts, axis=0))
    qs, ks, vs, _, _, mconv_s, sga_s, u_s = _rms_proj(
        xs_, attn_norm_g[l], w_bf, conv_w[l], prev_rows, t_len=ts, head_dim=hd)
    attn_s = _decode_attn(qs, ks, vs, cache_k, cache_v, l, page_table, rel_bias_table, lam,
                          subln_g[l], db=bs, tq=ts, lam_init=lam_init)

    rw_t = router_w[l].T
    rw_hi = rw_t.astype(BF16)
    rw_lo = (rw_t - rw_hi.astype(F32)).astype(BF16)
    rb = router_b[l].astype(F32).reshape(n_exp, 1)
    cnt0 = jnp.zeros((n_exp, 128), F32)
    x1p, h2p, idx_p, gate_p, rank_p, cnt1 = _merge_route(
        attn_p, sga_p, mconv_p, xp, wo_bf, ffn_norm_g[l], rw_hi, rw_lo, rb, cnt0)
    x1s, h2s, idx_s, gate_s, rank_s, cnt2 = _merge_route(
        attn_s, sga_s, mconv_s, xs_, wo_bf, ffn_norm_g[l], rw_hi, rw_lo, rb, cnt1)

    counts = cnt2[:, 0].astype(I32)
    padded = (counts + MOE_TM - 1) // MOE_TM * MOE_TM
    pad_end = jnp.cumsum(padded)
    pad_start = pad_end - padded
    n_blocks = -(-((np_ + ns_) * TOP_K) // MOE_TM) + n_exp
    block_start = jnp.arange(n_blocks, dtype=I32) * MOE_TM
    block_e = jnp.minimum(jnp.sum(block_start[:, None] >= pad_end[None, :], axis=1),
                          n_exp - 1).astype(I32)
    n_used = (pad_end[-1] // MOE_TM).astype(I32).reshape(1)
    dest_p = _expert_offsets(idx_p, rank_p, pad_start)
    dest_s = _expert_offsets(idx_s, rank_s, pad_start)

    xsorted = jnp.zeros((n_blocks * MOE_TM, d), F32)
    xsorted = _dispatch(h2p, dest_p, xsorted)
    xsorted = _dispatch(h2s, dest_s, xsorted)

    b_gu = b_gate_up[l].reshape(n_exp, ff, 2)
    yb = _moe_ffn(xsorted, block_e, n_used, wg, wu, wd,
                  b_gu[:, :, 0].reshape(n_exp, 1, ff), b_gu[:, :, 1].reshape(n_exp, 1, ff),
                  b_down[l].reshape(n_exp, 1, d))

    y_p = _combine(x1p, gate_p, dest_p, yb, final_norm_g)
    y_s = _combine(x1s, gate_s, dest_s, yb, final_norm_g)

    def stack(a, b, t):
        return a.reshape(depth, b, t, n_heads, vd)

    return (y_p.reshape(bp, tp, d), y_s.reshape(bs, ts, d),
            stack(kp, bp, tp), stack(vp, bp, tp), stack(ks, bs, ts), stack(vs, bs, ts),
            st_p.reshape(bp, -1, 2, d)[None, :, -1], u_s.reshape(bs, ts, d)[None, :, ts - 2:])
```
